```python
import math
import jax, jax.numpy as jnp
from jax import lax
import numpy as np

D_MODEL = 1024
BATCH = 8
SEQ = 4096
DEPTH = 4

HEAD_DIM = 64
N_HEADS = D_MODEL // HEAD_DIM
SWA_KV_HEADS = max(1, N_HEADS // 8)
N_MIXERS = 3
SB_BLOCK = 128
MOBA_BLOCK = 256
MOBA_TOPK = 3
MOBA_QCHUNK = 16
SWA_WINDOW = 128
ROPE_THETA = 10000.0
D_FF = -(-8 * D_MODEL // (3 * 256)) * 256
EPS = 1e-6

kernel_name = "hybrid_sb_moba_swa_adaln_trunk"


def rms_norm(x, gain):
    xf = x.astype(jnp.float32)
    var = jnp.mean(xf * xf, axis=-1, keepdims=True)
    return (xf * lax.rsqrt(var + EPS)).astype(x.dtype) * gain


def modulate(h, shift, scale):
    return h * (1.0 + scale) + shift


def split_heads(t, n):
    b, s, _ = t.shape
    return t.reshape(b, s, n, HEAD_DIM).transpose(0, 2, 1, 3)


def merge_heads(t):
    b, n, s, d = t.shape
    return t.transpose(0, 2, 1, 3).reshape(b, s, n * d)


def rope_tables(seq_len):
    inv_freq = 1.0 / (ROPE_THETA ** (jnp.arange(0, HEAD_DIM, 2, dtype=jnp.float32) / HEAD_DIM))
    ang = jnp.arange(seq_len, dtype=jnp.float32)[:, None] * inv_freq[None, :]
    return jnp.cos(ang), jnp.sin(ang)


def apply_rope(x, cos, sin):
    cos = cos.astype(x.dtype)
    sin = sin.astype(x.dtype)
    x1, x2 = jnp.split(x, 2, axis=-1)
    return jnp.concatenate([x1 * cos - x2 * sin, x2 * cos + x1 * sin], axis=-1)


def stick_breaking_attention(h, w_in, w_out):
    b, s_len, _ = h.shape
    q, k, v = jnp.split(h @ w_in, 3, axis=-1)
    q, k, v = split_heads(q, N_HEADS), split_heads(k, N_HEADS), split_heads(v, N_HEADS)
    nblk = s_len // SB_BLOCK
    q_blocks = q.reshape(b, N_HEADS, nblk, SB_BLOCK, HEAD_DIM).transpose(2, 0, 1, 3, 4)
    k_pos = jnp.arange(s_len)
    scale = HEAD_DIM ** -0.5

    def block(args):
        qb, bi = args
        q_pos = bi * SB_BLOCK + jnp.arange(SB_BLOCK)
        z = jnp.einsum('bhqd,bhkd->bhqk', qb, k).astype(jnp.float32) * scale
        past = k_pos[None, :] < q_pos[:, None]
        log_beta = jax.nn.log_sigmoid(z)
        log_keep = jnp.where(past, jax.nn.log_sigmoid(-z), 0.0)
        later = lax.cumsum(log_keep, axis=3, reverse=True) - log_keep
        a = jnp.where(past, jnp.exp(log_beta + later), 0.0)
        return jnp.einsum('bhqk,bhkd->bhqd', a.astype(v.dtype), v)

    o = lax.map(block, (q_blocks, jnp.arange(nblk)))
    o = o.transpose(1, 2, 0, 3, 4).reshape(b, N_HEADS, s_len, HEAD_DIM)
    return merge_heads(o) @ w_out


def moba_attention(h, w_in, qk_gain, w_out, cos, sin):
    b, s_len, _ = h.shape
    q, k, v = jnp.split(h @ w_in, 3, axis=-1)
    q, k, v = split_heads(q, N_HEADS), split_heads(k, N_HEADS), split_heads(v, N_HEADS)
    q = apply_rope(rms_norm(q, qk_gain[0]), cos, sin)
    k = apply_rope(rms_norm(k, qk_gain[1]), cos, sin)
    pad = (-s_len) % MOBA_BLOCK
    padw = ((0, 0), (0, 0), (0, pad), (0, 0))
    q, k, v = jnp.pad(q, padw), jnp.pad(k, padw), jnp.pad(v, padw)
    s_pad = s_len + pad
    nb = s_pad // MOBA_BLOCK
    topk = min(MOBA_TOPK, nb)
    k_blk = k.reshape(b, N_HEADS, nb, MOBA_BLOCK, HEAD_DIM)
    v_blk = v.reshape(b, N_HEADS, nb, MOBA_BLOCK, HEAD_DIM)
    k_mean = jnp.mean(k_blk.astype(jnp.float32), axis=3).astype(k.dtype)
    n_chunks = s_pad // MOBA_QCHUNK
    q_chunks = q.reshape(b, N_HEADS, n_chunks, MOBA_QCHUNK, HEAD_DIM).transpose(2, 0, 1, 3, 4)
    b_idx = jnp.arange(b)[:, None, None, None]
    h_idx = jnp.arange(N_HEADS)[None, :, None, None]
    blk_ids = jnp.arange(nb)
    in_blk = jnp.arange(MOBA_BLOCK)
    scale = HEAD_DIM ** -0.5

    def chunk(args):
        qc, ci = args
        q_pos = ci * MOBA_QCHUNK + jnp.arange(MOBA_QCHUNK)
        own = (ci * MOBA_QCHUNK) // MOBA_BLOCK
        gate = jnp.einsum('bhqd,bhnd->bhqn', qc, k_mean).astype(jnp.float32)
        gate = jnp.where(blk_ids < own, gate, -jnp.inf)
        top_val, top_idx = lax.top_k(gate, topk)
        sel_ok = jnp.isfinite(top_val)
        kg = k_blk[b_idx, h_idx, top_idx]
        vg = v_blk[b_idx, h_idx, top_idx]
        s_sel = jnp.einsum('bhqd,bhqnkd->bhqnk', qc, kg).astype(jnp.float32) * scale
        s_sel = jnp.where(sel_ok[..., None], s_sel, -jnp.inf)
        k_own = lax.dynamic_slice_in_dim(k, own * MOBA_BLOCK, MOBA_BLOCK, axis=2)
        v_own = lax.dynamic_slice_in_dim(v, own * MOBA_BLOCK, MOBA_BLOCK, axis=2)
        s_own = jnp.einsum('bhqd,bhkd->bhqk', qc, k_own).astype(jnp.float32) * scale
        own_pos = own * MOBA_BLOCK + in_blk
        s_own = jnp.where(own_pos[None, :] <= q_pos[:, None], s_own, -jnp.inf)
        qn = qc.shape[2]
        logits = jnp.concatenate([s_sel.reshape(b, N_HEADS, qn, topk * MOBA_BLOCK), s_own], axis=-1)
        p = jax.nn.softmax(logits, axis=-1).astype(v.dtype)
        p_sel = p[..., :topk * MOBA_BLOCK].reshape(b, N_HEADS, qn, topk, MOBA_BLOCK)
        p_own = p[..., topk * MOBA_BLOCK:]
        return (jnp.einsum('bhqnk,bhqnkd->bhqd', p_sel, vg)
                + jnp.einsum('bhqk,bhkd->bhqd', p_own, v_own))

    o = lax.map(chunk, (q_chunks, jnp.arange(n_chunks)))
    o = o.transpose(1, 2, 0, 3, 4).reshape(b, N_HEADS, s_pad, HEAD_DIM)[:, :, :s_len]
    return merge_heads(o) @ w_out


def swa_sink_attention(h, w_in, qk_gain, sinks, w_out, cos, sin):
    b, s_len, _ = h.shape
    qd, kvd = N_HEADS * HEAD_DIM, SWA_KV_HEADS * HEAD_DIM
    q, k, v = jnp.split(h @ w_in, [qd, qd + kvd], axis=-1)
    q, k, v = split_heads(q, N_HEADS), split_heads(k, SWA_KV_HEADS), split_heads(v, SWA_KV_HEADS)
    q = apply_rope(rms_norm(q, qk_gain[0]), cos, sin)
    k = apply_rope(rms_norm(k, qk_gain[1]), cos, sin)
    w = SWA_WINDOW
    nb = s_len // w
    g = N_HEADS // SWA_KV_HEADS
    qb = q.reshape(b, SWA_KV_HEADS, g, nb, w, HEAD_DIM)

    def band(t):
        tb = t.reshape(b, SWA_KV_HEADS, nb, w, HEAD_DIM)
        prev = jnp.pad(tb, ((0, 0), (0, 0), (1, 0), (0, 0), (0, 0)))[:, :, :-1]
        return jnp.concatenate([prev, tb], axis=3)

    kb, vb = band(k), band(v)
    s = jnp.einsum('bkgnqd,bknsd->bkgnqs', qb, kb).astype(jnp.float32) * (HEAD_DIM ** -0.5)
    q_off = jnp.arange(w)[:, None] + w
    k_off = jnp.arange(2 * w)[None, :]
    rel = q_off - k_off
    in_win = (rel >= 0) & (rel < SWA_WINDOW)
    valid = in_win[None] & ((jnp.arange(nb)[:, None, None] > 0) | (k_off >= w)[None])
    s = jnp.where(valid, s, -jnp.inf)
    sink = sinks.astype(jnp.float32).reshape(SWA_KV_HEADS, g)[None, :, :, None, None, None]
    m = jnp.maximum(jnp.max(s, axis=-1, keepdims=True), sink)
    e = jnp.exp(s - m)
    p = e / (jnp.sum(e, axis=-1, keepdims=True) + jnp.exp(sink - m))
    o = jnp.einsum('bkgnqs,bknsd->bkgnqd', p.astype(v.dtype), vb)
    o = o.reshape(b, N_HEADS, s_len, HEAD_DIM)
    return merge_heads(o) @ w_out


def swiglu(h, w_gate, w_up, w_down):
    return (jax.nn.silu(h @ w_gate) * (h @ w_up)) @ w_down


def setup_inputs(seed: int = 0) -> dict:
    key = jax.random.key(seed)
    ks = jax.random.split(key, 20)
    d = D_MODEL
    hd = N_HEADS * HEAD_DIM
    n_sb, n_moba, n_swa = (DEPTH + 2) // 3, (DEPTH + 1) // 3, DEPTH // 3

    def nrm(k, shape, scale):
        return jax.random.normal(k, shape, jnp.float32) * scale

    return {
        "x": nrm(ks[0], (BATCH, SEQ, d), 1.0),
        "c": nrm(ks[1], (BATCH, d), 1.0),
        "ada_w": nrm(ks[2], (DEPTH, d, 6 * d), 0.5 * d ** -0.5),
        "ada_b": nrm(ks[3], (DEPTH, 6 * d), 0.02),
        "norm_gain": 1.0 + nrm(ks[4], (DEPTH, 2, d), 0.02),
        "ffn_w_gate": nrm(ks[5], (DEPTH, d, D_FF), d ** -0.5),
        "ffn_w_up": nrm(ks[6], (DEPTH, d, D_FF), d ** -0.5),
        "ffn_w_down": nrm(ks[7], (DEPTH, D_FF, d), D_FF ** -0.5),
        "sb_w_in": nrm(ks[8], (n_sb, d, 3 * hd), d ** -0.5),
        "sb_w_out": nrm(ks[9], (n_sb, hd, d), hd ** -0.5),
        "moba_w_in": nrm(ks[10], (n_moba, d, 3 * hd), d ** -0.5),
        "moba_qk_gain": 1.0 + nrm(ks[11], (n_moba, 2, HEAD_DIM), 0.02),
        "moba_w_out": nrm(ks[12], (n_moba, hd, d), hd ** -0.5),
        "swa_w_in": nrm(ks[13], (n_swa, d, (N_HEADS + 2 * SWA_KV_HEADS) * HEAD_DIM), d ** -0.5),
        "swa_qk_gain": 1.0 + nrm(ks[14], (n_swa, 2, HEAD_DIM), 0.02),
        "swa_sinks": nrm(ks[15], (n_swa, N_HEADS), 0.5),
        "swa_w_out": nrm(ks[16], (n_swa, hd, d), hd ** -0.5),
    }


def reference(x, c, ada_w, ada_b, norm_gain, ffn_w_gate, ffn_w_up, ffn_w_down,
              sb_w_in, sb_w_out, moba_w_in, moba_qk_gain, moba_w_out,
              swa_w_in, swa_qk_gain, swa_sinks, swa_w_out):
    s_len = x.shape[1]
    cos, sin = rope_tables(s_len)
    c_act = jax.nn.silu(c)
    for i in range(DEPTH):
        mod = c_act @ ada_w[i] + ada_b[i]
        sh1, sc1, g1, sh2, sc2, g2 = [m[:, None, :] for m in jnp.split(mod, 6, axis=-1)]
        h = modulate(rms_norm(x, norm_gain[i, 0]), sh1, sc1)
        kind, j = i % N_MIXERS, i // N_MIXERS
        if kind == 0:
            y = stick_breaking_attention(h, sb_w_in[j], sb_w_out[j])
        elif kind == 1:
            y = moba_attention(h, moba_w_in[j], moba_qk_gain[j], moba_w_out[j], cos, sin)
        else:
            y = swa_sink_attention(h, swa_w_in[j], swa_qk_gain[j], swa_sinks[j], swa_w_out[j], cos, sin)
        x = x + g1 * y
        h = modulate(rms_norm(x, norm_gain[i, 1]), sh2, sc2)
        x = x + g2 * swiglu(h, ffn_w_gate[i], ffn_w_up[i], ffn_w_down[i])
    return x
```

```python
import functools

import numpy as np
import jax
import jax.numpy as jnp
from jax import lax
from jax.experimental import pallas as pl
from jax.experimental.pallas import tpu as pltpu

D_MODEL = 1024
HEAD_DIM = 64
N_HEADS = D_MODEL // HEAD_DIM
SWA_KV_HEADS = 2
N_MIXERS = 3
MOBA_BLOCK = 256
MOBA_TOPK = 3
SWA_WINDOW = 128
ROPE_THETA = 10000.0
EPS = 1e-6

LANES = 128
N_QBLK = D_MODEL // LANES
VMEM_LIMIT_BYTES = 56 * 1024 * 1024
SB_BLOCK = 256
SB_STOP = 110.0
ROW_TILE = 512
FFN_ROW_TILE = 256

BF16 = jnp.bfloat16
F32 = jnp.float32
NT_DIMS = (((1,), (1,)), ((), ()))


def _compiler_params(n_grid):
    return pltpu.CompilerParams(
        dimension_semantics=("arbitrary",) * n_grid,
        vmem_limit_bytes=VMEM_LIMIT_BYTES)


def _resident(shape):
    nd = len(shape)
    return pl.BlockSpec(shape, lambda *_: (0,) * nd, pipeline_mode=pl.Buffered(1))


def _split_bf16(a):
    hi = a.astype(BF16)
    lo = (a - hi.astype(F32)).astype(BF16)
    return hi, lo


def _ada_kernel(c_ref, w_ref, b_ref, o_ref):
    c = c_ref[...]
    c_act = c * (1.0 / (1.0 + jnp.exp(-c)))
    o_ref[0] = jnp.dot(c_act, w_ref[0], preferred_element_type=F32,
                       precision=lax.Precision.HIGHEST) + b_ref[0]


def _ada_modulation(c, ada_w, ada_b):
    depth, d, n = ada_w.shape
    b = c.shape[0]
    tn = n // 4
    return pl.pallas_call(
        _ada_kernel,
        out_shape=jax.ShapeDtypeStruct((depth, b, n), F32),
        grid=(depth, n // tn),
        in_specs=[
            pl.BlockSpec((b, d), lambda l, j: (0, 0)),
            pl.BlockSpec((1, d, tn), lambda l, j: (l, 0, j)),
            pl.BlockSpec((1, 1, tn), lambda l, j: (l, 0, j)),
        ],
        out_specs=pl.BlockSpec((1, b, tn), lambda l, j: (l, 0, j)),
        compiler_params=_compiler_params(2),
        name="ada_modulation",
    )(c, ada_w, ada_b.reshape(depth, 1, n))


def _norm_modulate(x, gain, shift, scale):
    var = jnp.mean(x * x, axis=-1, keepdims=True)
    h = (x * lax.rsqrt(var + EPS)) * gain
    return h * (1.0 + scale) + shift


def _norm_proj_kernel(*refs, n_rope, tk):
    if n_rope:
        (x_ref, mod_ref, gain_ref, wqk_ref, wvt_ref,
         cos_ref, sin_ref, gsum_ref, qkg_ref, oqk_ref, ovt_ref) = refs
    else:
        x_ref, mod_ref, gain_ref, wqk_ref, wvt_ref, oqk_ref, ovt_ref = refs
    x = x_ref[0]
    h = _norm_modulate(x, gain_ref[...], mod_ref[0, 0, 0:1, :], mod_ref[0, 0, 1:2, :])
    hb = h.astype(BF16)
    vt = lax.dot_general(wvt_ref[...], hb, NT_DIMS, preferred_element_type=F32)
    tm = x.shape[0]
    for t in range(tm // tk):
        ovt_ref[0, t] = vt[:, t * tk:(t + 1) * tk].astype(BF16)
    qk = jnp.dot(hb, wqk_ref[...], preferred_element_type=F32)
    if not n_rope:
        oqk_ref[0] = qk.astype(BF16)
        return
    cos = cos_ref[...]
    sin = sin_ref[...]
    gsum = gsum_ref[...]
    for cb in range(n_rope):
        a = qk[:, cb * LANES:(cb + 1) * LANES]
        hi, lo = _split_bf16(a * a)
        ss = jnp.dot(jnp.concatenate([hi, lo], axis=1), gsum, preferred_element_type=F32)
        r = lax.rsqrt(ss * (1.0 / HEAD_DIM) + EPS)
        g = qkg_ref[0:1, :] if cb < N_QBLK else qkg_ref[1:2, :]
        xn = (a * r) * g
        out = xn * cos + pltpu.roll(xn, LANES // 2, 1) * sin
        oqk_ref[0, :, cb * LANES:(cb + 1) * LANES] = out.astype(BF16)


def _norm_proj(x, mod, layer, gain, wqk, wvt, tk, rope=None):
    b, s, d = x.shape
    nqk = wqk.shape[1]
    nv = wvt.shape[0]
    tm = min(ROW_TILE, s)
    n_rope = 0 if rope is None else nqk // LANES
    in_specs = [
        pl.BlockSpec((1, tm, d), lambda bi, i: (bi, i, 0)),
        pl.BlockSpec((1, 1, 6, d), lambda bi, i: (layer, bi, 0, 0)),
        _resident((1, d)),
        _resident((d, nqk)),
        _resident((nv, d)),
    ]
    args = [x, mod, gain.reshape(1, d), wqk, wvt]
    if rope is not None:
        cos_t, sin_t, gsum, qkg = rope
        in_specs += [
            pl.BlockSpec((tm, LANES), lambda bi, i: (i, 0)),
            pl.BlockSpec((tm, LANES), lambda bi, i: (i, 0)),
            _resident((2 * LANES, LANES)),
            _resident((2, LANES)),
        ]
        args += [cos_t, sin_t, gsum, qkg]
    return pl.pallas_call(
        functools.partial(_norm_proj_kernel, n_rope=n_rope, tk=tk),
        out_shape=(jax.ShapeDtypeStruct((b, s, nqk), BF16),
                   jax.ShapeDtypeStruct((b, s // tk, nv, tk), BF16)),
        grid=(b, s // tm),
        in_specs=in_specs,
        out_specs=(pl.BlockSpec((1, tm, nqk), lambda bi, i: (bi, i, 0)),
                   pl.BlockSpec((1, tm // tk, nv, tk), lambda bi, i: (bi, i, 0, 0))),
        compiler_params=_compiler_params(2),
        name=f"norm_proj_l{layer}",
    )(*args)


def _load_q_heads(q_ref, interleaved):
    q = (q_ref[0].astype(F32) * (HEAD_DIM ** -0.5)).astype(BF16)
    lane = lax.broadcasted_iota(jnp.int32, q.shape, 1)
    first = ((lane // (HEAD_DIM // 2)) % 2 == 0) if interleaved else (lane < HEAD_DIM)
    zero = jnp.zeros_like(q)
    return jnp.where(first, q, zero), jnp.where(first, zero, q)


def _store_heads(o_ref, o_a, o_b):
    o_ref[0] = jnp.concatenate([o_a, o_b], axis=0).T.astype(o_ref.dtype)


def _sb_attn_kernel(q_ref, k_ref, vt_ref, lmat_ref, o_ref, *, blk):
    i = pl.program_id(2)
    qs = _load_q_heads(q_ref, interleaved=False)
    lmat = lmat_ref[...]
    krow = lax.broadcasted_iota(jnp.int32, (blk, blk), 0)
    qcol = lax.broadcasted_iota(jnp.int32, (blk, blk), 1)
    past = krow < qcol

    def block(j, carries, accs, diag):
        kj = k_ref[0, pl.ds(pl.multiple_of(j * blk, blk), blk), :]
        vj = vt_ref[0, j]
        new_c, new_a = [], []
        for hd in range(2):
            z = lax.dot_general(kj, qs[hd], NT_DIMS, preferred_element_type=F32)
            sp = jnp.log(1.0 + jnp.exp(-jnp.abs(z)))
            log_keep = -(jnp.maximum(z, 0.0) + sp)
            log_beta = log_keep + z
            if diag:
                log_keep = jnp.where(past, log_keep, 0.0)
            hi, lo = _split_bf16(log_keep)
            later = (jnp.dot(lmat, hi, preferred_element_type=F32)
                     + jnp.dot(lmat, lo, preferred_element_type=F32))
            a = jnp.exp(log_beta + later + carries[hd])
            if diag:
                a = jnp.where(past, a, 0.0)
            v_h = vj[hd * HEAD_DIM:(hd + 1) * HEAD_DIM, :]
            new_a.append(accs[hd] + jnp.dot(v_h, a.astype(BF16), preferred_element_type=F32))
            new_c.append(carries[hd] + jnp.sum(log_keep, axis=0, keepdims=True))
        return new_c, new_a

    def live(carries):
        return (jnp.max(jnp.maximum(carries[0], carries[1])) > -SB_STOP).astype(jnp.int32)

    zero_c = jnp.zeros((1, blk), F32)
    zero_a = jnp.zeros((HEAD_DIM, blk), F32)
    carries, accs = block(i, [zero_c, zero_c], [zero_a, zero_a], diag=True)

    def cond(st):
        return jnp.logical_and(st[0] >= 0, st[1] > 0)

    def body(st):
        j, _, c0, c1, a0, a1 = st
        cs, as_ = block(j, [c0, c1], [a0, a1], diag=False)
        return (j - 1, live(cs), cs[0], cs[1], as_[0], as_[1])

    st = lax.while_loop(cond, body, (i - 1, live(carries), carries[0], carries[1], accs[0], accs[1]))
    _store_heads(o_ref, st[4], st[5])


def _sb_attention(qk, vt, blk):
    b, s, _ = qk.shape
    nb = s // blk
    row = np.arange(blk)
    lmat = jnp.asarray(row[None, :] > row[:, None], BF16)
    return pl.pallas_call(
        functools.partial(_sb_attn_kernel, blk=blk),
        out_shape=jax.ShapeDtypeStruct((b, s, D_MODEL), BF16),
        grid=(b, N_QBLK, nb),
        in_specs=[
            pl.BlockSpec((1, blk, LANES), lambda bi, c, i: (bi, i, c)),
            pl.BlockSpec((1, s, LANES), lambda bi, c, i: (bi, 0, N_QBLK + c)),
            pl.BlockSpec((1, nb, LANES, blk), lambda bi, c, i: (bi, 0, c, 0)),
            _resident((blk, blk)),
        ],
        out_specs=pl.BlockSpec((1, blk, LANES), lambda bi, c, i: (bi, i, c)),
        compiler_params=_compiler_params(3),
        name="sb_attention",
    )(qk, qk, vt, lmat)


def _online_softmax_step(s, m, l, acc, v_h):
    m_new = jnp.maximum(m, jnp.max(s, axis=0, keepdims=True))
    p = jnp.exp(s - m_new)
    alpha = jnp.exp(m - m_new)
    l_new = alpha * l + jnp.sum(p, axis=0, keepdims=True)
    acc_new = alpha * acc + jnp.dot(v_h, p.astype(BF16), preferred_element_type=F32)
    return m_new, l_new, acc_new


def _moba_attn_kernel(q_ref, k_ref, vt_ref, avg_ref, o_ref, kmean_ref, *, blk):
    i = pl.program_id(2)
    nbp = avg_ref.shape[0]

    @pl.when(i == 0)
    def _():
        kmean_ref[...] = jnp.dot(avg_ref[...], k_ref[0], preferred_element_type=F32)

    qs = _load_q_heads(q_ref, interleaved=True)
    krow = lax.broadcasted_iota(jnp.int32, (blk, blk), 0)
    qcol = lax.broadcasted_iota(jnp.int32, (blk, blk), 1)
    causal = krow <= qcol
    nidx = lax.broadcasted_iota(jnp.int32, (nbp, blk), 0)
    neg_inf = jnp.float32(-jnp.inf)
    km_hi, km_lo = _split_bf16(kmean_ref[...])

    picks = []
    for hd in range(2):
        gate = (lax.dot_general(km_hi, qs[hd], NT_DIMS, preferred_element_type=F32)
                + lax.dot_general(km_lo, qs[hd], NT_DIMS, preferred_element_type=F32))
        gate = jnp.where(nidx < i, gate, neg_inf)
        chosen = []
        for _ in range(MOBA_TOPK):
            top = jnp.max(gate, axis=0, keepdims=True)
            idx = jnp.min(jnp.where(gate == top, nidx, nbp), axis=0, keepdims=True)
            chosen.append(jnp.where(top > neg_inf, idx, -1))
            gate = jnp.where(nidx == idx, neg_inf, gate)
        picks.append(chosen)

    k_own = k_ref[0, pl.ds(pl.multiple_of(i * blk, blk), blk), :]
    v_own = vt_ref[0, i]
    state = []
    for hd in range(2):
        s = lax.dot_general(k_own, qs[hd], NT_DIMS, preferred_element_type=F32)
        s = jnp.where(causal, s, neg_inf)
        m0 = jnp.full((1, blk), neg_inf, F32)
        l0 = jnp.zeros((1, blk), F32)
        a0 = jnp.zeros((HEAD_DIM, blk), F32)
        state.extend(_online_softmax_step(s, m0, l0, a0, v_own[hd * HEAD_DIM:(hd + 1) * HEAD_DIM, :]))

    def body(j, st):
        kj = k_ref[0, pl.ds(pl.multiple_of(j * blk, blk), blk), :]
        vj = vt_ref[0, j]
        out = []
        for hd in range(2):
            m, l, acc = st[3 * hd:3 * hd + 3]
            sel = functools.reduce(jnp.logical_or, [p == j for p in picks[hd]])
            s = lax.dot_general(kj, qs[hd], NT_DIMS, preferred_element_type=F32)
            s = jnp.where(sel, s, neg_inf)
            out.extend(_online_softmax_step(s, m, l, acc, vj[hd * HEAD_DIM:(hd + 1) * HEAD_DIM, :]))
        return tuple(out)

    st = lax.fori_loop(0, i, body, tuple(state))
    _store_heads(o_ref, st[2] / st[1], st[5] / st[4])


def _moba_attention(qk, vt):
    b, s, _ = qk.shape
    blk = MOBA_BLOCK
    nb = s // blk
    nbp = -(-nb // 8) * 8
    blk_of_key = np.arange(s) // blk
    avg = jnp.asarray((np.arange(nbp)[:, None] == blk_of_key[None, :]) * (1.0 / blk), BF16)
    return pl.pallas_call(
        functools.partial(_moba_attn_kernel, blk=blk),
        out_shape=jax.ShapeDtypeStruct((b, s, D_MODEL), BF16),
        grid=(b, N_QBLK, nb),
        in_specs=[
            pl.BlockSpec((1, blk, LANES), lambda bi, c, i: (bi, i, c)),
            pl.BlockSpec((1, s, LANES), lambda bi, c, i: (bi, 0, N_QBLK + c)),
            pl.BlockSpec((1, nb, LANES, blk), lambda bi, c, i: (bi, 0, c, 0)),
            _resident((nbp, s)),
        ],
        out_specs=pl.BlockSpec((1, blk, LANES), lambda bi, c, i: (bi, i, c)),
        scratch_shapes=[pltpu.VMEM((nbp, LANES), F32)],
        compiler_params=_compiler_params(3),
        name="moba_attention",
    )(qk, qk, vt, avg)


def _swa_attn_kernel(sink_ref, q_ref, k_ref, vt_ref, o_ref, *, blk):
    c = pl.program_id(1)
    i = pl.program_id(2)
    qs = _load_q_heads(q_ref, interleaved=True)
    krow = lax.broadcasted_iota(jnp.int32, (blk, blk), 0)
    qcol = lax.broadcasted_iota(jnp.int32, (blk, blk), 1)
    neg_inf = jnp.float32(-jnp.inf)
    in_own = krow <= qcol
    in_prev = jnp.logical_and(krow > qcol, i > 0)
    jp = jnp.maximum(i - 1, 0)
    k_own = k_ref[0, pl.ds(pl.multiple_of(i * blk, blk), blk), :]
    k_prev = k_ref[0, pl.ds(pl.multiple_of(jp * blk, blk), blk), :]
    v_own = vt_ref[0, i]
    v_prev = vt_ref[0, jp]
    outs = []
    for hd in range(2):
        sink = sink_ref[c + hd * N_QBLK]
        rows = slice(hd * HEAD_DIM, (hd + 1) * HEAD_DIM)
        s1 = jnp.where(in_own, lax.dot_general(k_own, qs[hd], NT_DIMS, preferred_element_type=F32), neg_inf)
        s0 = jnp.where(in_prev, lax.dot_general(k_prev, qs[hd], NT_DIMS, preferred_element_type=F32), neg_inf)
        m = jnp.maximum(jnp.maximum(jnp.max(s0, axis=0, keepdims=True),
                                    jnp.max(s1, axis=0, keepdims=True)), sink)
        p0 = jnp.exp(s0 - m)
        p1 = jnp.exp(s1 - m)
        denom = (jnp.sum(p0, axis=0, keepdims=True) + jnp.sum(p1, axis=0, keepdims=True)
                 + jnp.exp(sink - m))
        o = (jnp.dot(v_prev[rows, :], p0.astype(BF16), preferred_element_type=F32)
             + jnp.dot(v_own[rows, :], p1.astype(BF16), preferred_element_type=F32))
        outs.append(o / denom)
    _store_heads(o_ref, outs[0], outs[1])


def _swa_attention(qk, vt, sinks):
    b, s, _ = qk.shape
    blk = SWA_WINDOW
    nb = s // blk
    return pl.pallas_call(
        functools.partial(_swa_attn_kernel, blk=blk),
        out_shape=jax.ShapeDtypeStruct((b, s, D_MODEL), BF16),
        grid=(b, N_QBLK, nb),
        in_specs=[
            pl.BlockSpec(memory_space=pltpu.SMEM),
            pl.BlockSpec((1, blk, LANES), lambda bi, c, i: (bi, i, c)),
            pl.BlockSpec((1, s, LANES), lambda bi, c, i: (bi, 0, N_QBLK)),
            pl.BlockSpec((1, nb, LANES, blk), lambda bi, c, i: (bi, 0, 0, 0)),
        ],
        out_specs=pl.BlockSpec((1, blk, LANES), lambda bi, c, i: (bi, i, c)),
        compiler_params=_compiler_params(3),
        name="swa_attention",
    )(sinks, qk, qk, vt)


def _out_ffn_kernel(attn_ref, x_ref, mod_ref, gain_ref, wo_ref, wg_ref, wu_ref, wd_ref, o_ref):
    y = jnp.dot(attn_ref[0], wo_ref[...], preferred_element_type=F32)
    x1 = x_ref[0] + mod_ref[0, 0, 2:3, :] * y
    h = _norm_modulate(x1, gain_ref[...], mod_ref[0, 0, 3:4, :], mod_ref[0, 0, 4:5, :])
    hb = h.astype(BF16)
    gate = jnp.dot(hb, wg_ref[...], preferred_element_type=F32)
    up = jnp.dot(hb, wu_ref[...], preferred_element_type=F32)
    act = (gate * (1.0 / (1.0 + jnp.exp(-gate)))) * up
    ff = jnp.dot(act.astype(BF16), wd_ref[...], preferred_element_type=F32)
    o_ref[0] = x1 + mod_ref[0, 0, 5:6, :] * ff


def _out_ffn(attn, x, mod, layer, gain, wo, wg, wu, wd):
    b, s, d = x.shape
    dff = wg.shape[1]
    tm = min(FFN_ROW_TILE, s)
    return pl.pallas_call(
        _out_ffn_kernel,
        out_shape=jax.ShapeDtypeStruct((b, s, d), F32),
        grid=(b, s // tm),
        in_specs=[
            pl.BlockSpec((1, tm, d), lambda bi, i: (bi, i, 0)),
            pl.BlockSpec((1, tm, d), lambda bi, i: (bi, i, 0)),
            pl.BlockSpec((1, 1, 6, d), lambda bi, i: (layer, bi, 0, 0)),
            _resident((1, d)),
            _resident((d, d)),
            _resident((d, dff)),
            _resident((d, dff)),
            _resident((dff, d)),
        ],
        out_specs=pl.BlockSpec((1, tm, d), lambda bi, i: (bi, i, 0)),
        compiler_params=_compiler_params(2),
        name=f"out_ffn_l{layer}",
    )(attn, x, mod, gain.reshape(1, d), wo, wg, wu, wd)


def _interleave_perm(head_a, head_b):
    half = HEAD_DIM // 2
    r = np.arange(half)
    return np.concatenate([head_a * HEAD_DIM + r, head_b * HEAD_DIM + r,
                           head_a * HEAD_DIM + half + r, head_b * HEAD_DIM + half + r])


def _rope_tables(seq_len):
    half = HEAD_DIM // 2
    inv_freq = 1.0 / (ROPE_THETA ** (jnp.arange(0, HEAD_DIM, 2, dtype=F32) / HEAD_DIM))
    ang = jnp.arange(seq_len, dtype=F32)[:, None] * inv_freq[None, :]
    cos, sin = jnp.cos(ang), jnp.sin(ang)
    return (jnp.concatenate([cos] * 4, axis=1),
            jnp.concatenate([-sin, -sin, sin, sin], axis=1))


def _head_sum_matrix():
    half = HEAD_DIM // 2
    head = (np.arange(LANES) // half) % 2
    g = (head[:, None] == head[None, :]).astype(np.float32)
    return jnp.asarray(np.concatenate([g, g], axis=0), BF16)


def _block_gain(qk_gain):
    half = HEAD_DIM // 2
    lane = np.arange(LANES)
    dim = (lane % half) + half * (lane // (2 * half))
    return qk_gain[:, dim]


def kernel(x, c, ada_w, ada_b, norm_gain, ffn_w_gate, ffn_w_up, ffn_w_down, sb_w_in, sb_w_out, moba_w_in, moba_qk_gain, moba_w_out, swa_w_in, swa_qk_gain, swa_sinks, swa_w_out):
    b, s, d = x.shape
    depth = ada_w.shape[0]
    hd_all = N_HEADS * HEAD_DIM
    mod = _ada_modulation(c, ada_w, ada_b).reshape(depth, b, 6, d)
    cos_t, sin_t = _rope_tables(s)
    gsum = _head_sum_matrix()

    moba_cols = np.concatenate(
        [_interleave_perm(2 * cb, 2 * cb + 1) for cb in range(N_QBLK)]
        + [hd_all + _interleave_perm(2 * cb, 2 * cb + 1) for cb in range(N_QBLK)])
    swa_cols = np.concatenate(
        [_interleave_perm(cb, N_QBLK + cb) for cb in range(N_QBLK)]
        + [hd_all + _interleave_perm(0, 1)])
    swa_out_rows = np.concatenate(
        [np.concatenate([np.arange(HEAD_DIM) + cb * HEAD_DIM,
                         np.arange(HEAD_DIM) + (N_QBLK + cb) * HEAD_DIM]) for cb in range(N_QBLK)])

    for i in range(depth):
        kind, j = i % N_MIXERS, i // N_MIXERS
        if kind == 0:
            w_in = sb_w_in[j]
            wqk = w_in[:, :2 * hd_all].astype(BF16)
            wvt = w_in[:, 2 * hd_all:].T.astype(BF16)
            qk, vt = _norm_proj(x, mod, i, norm_gain[i, 0], wqk, wvt, SB_BLOCK)
            attn = _sb_attention(qk, vt, SB_BLOCK)
            wo = sb_w_out[j].astype(BF16)
        elif kind == 1:
            w_in = moba_w_in[j]
            wqk = w_in[:, moba_cols].astype(BF16)
            wvt = w_in[:, 2 * hd_all:].T.astype(BF16)
            rope = (cos_t, sin_t, gsum, _block_gain(moba_qk_gain[j]))
            qk, vt = _norm_proj(x, mod, i, norm_gain[i, 0], wqk, wvt, MOBA_BLOCK, rope)
            attn = _moba_attention(qk, vt)
            wo = moba_w_out[j].astype(BF16)
        else:
            w_in = swa_w_in[j]
            wqk = w_in[:, swa_cols].astype(BF16)
            wvt = w_in[:, hd_all + SWA_KV_HEADS * HEAD_DIM:].T.astype(BF16)
            rope = (cos_t, sin_t, gsum, _block_gain(swa_qk_gain[j]))
            qk, vt = _norm_proj(x, mod, i, norm_gain[i, 0], wqk, wvt, SWA_WINDOW, rope)
            attn = _swa_attention(qk, vt, swa_sinks[j])
            wo = swa_w_out[j][swa_out_rows, :].astype(BF16)
        x = _out_ffn(attn, x, mod, i, norm_gain[i, 1], wo,
                     ffn_w_gate[i].astype(BF16), ffn_w_up[i].astype(BF16), ffn_w_down[i].astype(BF16))
    return x
```

```python
import functools

import numpy as np
import jax
import jax.numpy as jnp
from jax import lax
from jax.experimental import pallas as pl
from jax.experimental.pallas import tpu as pltpu

D_MODEL = 1024
HEAD_DIM = 64
N_HEADS = D_MODEL // HEAD_DIM
SWA_KV_HEADS = 2
N_MIXERS = 3
MOBA_BLOCK = 256
MOBA_TOPK = 3
SWA_WINDOW = 128
ROPE_THETA = 10000.0
EPS = 1e-6

LANES = 128
N_QBLK = D_MODEL // LANES
VMEM_LIMIT_BYTES = 56 * 1024 * 1024
SB_BLOCK = 256
SB_STOP = 110.0
SB_COL_BLOCKS = 4
MOBA_TILE_BLOCKS = 2
MOBA_COL_BLOCKS = 2
SWA_Q_TILE = 512
ROW_TILE = 512
FFN_ROW_TILE = 256

BF16 = jnp.bfloat16
F32 = jnp.float32
NT_DIMS = (((1,), (1,)), ((), ()))


def _compiler_params(n_grid):
    return pltpu.CompilerParams(
        dimension_semantics=("arbitrary",) * n_grid,
        vmem_limit_bytes=VMEM_LIMIT_BYTES)


def _resident(shape):
    nd = len(shape)
    return pl.BlockSpec(shape, lambda *_: (0,) * nd, pipeline_mode=pl.Buffered(1))


def _split_bf16(a):
    hi = a.astype(BF16)
    lo = (a - hi.astype(F32)).astype(BF16)
    return hi, lo


def _ada_kernel(c_ref, w_ref, b_ref, o_ref):
    c = c_ref[...]
    c_act = c * (1.0 / (1.0 + jnp.exp(-c)))
    o_ref[0] = jnp.dot(c_act, w_ref[0], preferred_element_type=F32,
                       precision=lax.Precision.HIGHEST) + b_ref[0]


def _ada_modulation(c, ada_w, ada_b):
    depth, d, n = ada_w.shape
    b = c.shape[0]
    tn = n // 4
    return pl.pallas_call(
        _ada_kernel,
        out_shape=jax.ShapeDtypeStruct((depth, b, n), F32),
        grid=(depth, n // tn),
        in_specs=[
            pl.BlockSpec((b, d), lambda l, j: (0, 0)),
            pl.BlockSpec((1, d, tn), lambda l, j: (l, 0, j)),
            pl.BlockSpec((1, 1, tn), lambda l, j: (l, 0, j)),
        ],
        out_specs=pl.BlockSpec((1, b, tn), lambda l, j: (l, 0, j)),
        compiler_params=_compiler_params(2),
        name="ada_modulation",
    )(c, ada_w, ada_b.reshape(depth, 1, n))


def _norm_modulate(x, gain, shift, scale):
    var = jnp.mean(x * x, axis=-1, keepdims=True)
    h = (x * lax.rsqrt(var + EPS)) * gain
    return h * (1.0 + scale) + shift


def _norm_proj_kernel(*refs, n_rope, tk):
    if n_rope:
        (x_ref, mod_ref, gain_ref, wqk_ref, wvt_ref,
         cos_ref, sin_ref, gsum_ref, qkg_ref, oqk_ref, ovt_ref) = refs
    else:
        x_ref, mod_ref, gain_ref, wqk_ref, wvt_ref, oqk_ref, ovt_ref = refs
    x = x_ref[0]
    h = _norm_modulate(x, gain_ref[...], mod_ref[0, 0, 0:1, :], mod_ref[0, 0, 1:2, :])
    hb = h.astype(BF16)
    vt = lax.dot_general(wvt_ref[...], hb, NT_DIMS, preferred_element_type=F32)
    tm = x.shape[0]
    for t in range(tm // tk):
        ovt_ref[0, t] = vt[:, t * tk:(t + 1) * tk].astype(BF16)
    qk = jnp.dot(hb, wqk_ref[...], preferred_element_type=F32)
    if not n_rope:
        oqk_ref[0] = qk.astype(BF16)
        return
    cos = cos_ref[...]
    sin = sin_ref[...]
    gsum = gsum_ref[...]
    for cb in range(n_rope):
        a = qk[:, cb * LANES:(cb + 1) * LANES]
        hi, lo = _split_bf16(a * a)
        ss = jnp.dot(jnp.concatenate([hi, lo], axis=1), gsum, preferred_element_type=F32)
        r = lax.rsqrt(ss * (1.0 / HEAD_DIM) + EPS)
        g = qkg_ref[0:1, :] if cb < N_QBLK else qkg_ref[1:2, :]
        xn = (a * r) * g
        out = xn * cos + pltpu.roll(xn, LANES // 2, 1) * sin
        oqk_ref[0, :, cb * LANES:(cb + 1) * LANES] = out.astype(BF16)


def _norm_proj(x, mod, layer, gain, wqk, wvt, tk, rope=None):
    b, s, d = x.shape
    nqk = wqk.shape[1]
    nv = wvt.shape[0]
    tm = min(ROW_TILE, s)
    n_rope = 0 if rope is None else nqk // LANES
    in_specs = [
        pl.BlockSpec((1, tm, d), lambda bi, i: (bi, i, 0)),
        pl.BlockSpec((1, 1, 6, d), lambda bi, i: (layer, bi, 0, 0)),
        _resident((1, d)),
        _resident((d, nqk)),
        _resident((nv, d)),
    ]
    args = [x, mod, gain.reshape(1, d), wqk, wvt]
    if rope is not None:
        cos_t, sin_t, gsum, qkg = rope
        in_specs += [
            pl.BlockSpec((tm, LANES), lambda bi, i: (i, 0)),
            pl.BlockSpec((tm, LANES), lambda bi, i: (i, 0)),
            _resident((2 * LANES, LANES)),
            _resident((2, LANES)),
        ]
        args += [cos_t, sin_t, gsum, qkg]
    return pl.pallas_call(
        functools.partial(_norm_proj_kernel, n_rope=n_rope, tk=tk),
        out_shape=(jax.ShapeDtypeStruct((b, s, nqk), BF16),
                   jax.ShapeDtypeStruct((b, s // tk, nv, tk), BF16)),
        grid=(b, s // tm),
        in_specs=in_specs,
        out_specs=(pl.BlockSpec((1, tm, nqk), lambda bi, i: (bi, i, 0)),
                   pl.BlockSpec((1, tm // tk, nv, tk), lambda bi, i: (bi, i, 0, 0))),
        compiler_params=_compiler_params(2),
        name=f"norm_proj_l{layer}",
    )(*args)


def _stack_heads(q, interleaved, scale=HEAD_DIM ** -0.5):
    q = (q.astype(F32) * scale).astype(BF16)
    lane = lax.broadcasted_iota(jnp.int32, q.shape, 1)
    first = ((lane // (HEAD_DIM // 2)) % 2 == 0) if interleaved else (lane < HEAD_DIM)
    zero = jnp.zeros_like(q)
    return jnp.concatenate([jnp.where(first, q, zero), jnp.where(first, zero, q)], axis=0)


def _own_heads(o, tq):
    return o[:HEAD_DIM, :tq], o[HEAD_DIM:, tq:]


def _store_heads(o_ref, rows, o_a, o_b):
    o_ref[0, rows, :] = jnp.concatenate([o_a, o_b], axis=0).T.astype(o_ref.dtype)


def _key_query_iota(blk):
    krow = lax.broadcasted_iota(jnp.int32, (blk, 2 * blk), 0)
    qcol = lax.broadcasted_iota(jnp.int32, (blk, 2 * blk), 1)
    return krow, jnp.where(qcol >= blk, qcol - blk, qcol)


def _colsum(a):
    return jnp.sum(a, axis=0, keepdims=True)


def _sb_attn_kernel(q_ref, k_ref, vt_ref, lmat_ref, o_ref, *, blk, ncb):
    i = pl.program_id(2)
    cols = [slice(cb * LANES, (cb + 1) * LANES) for cb in range(ncb)]
    nq2 = [_stack_heads(q_ref[0, :, c], interleaved=False, scale=-(HEAD_DIM ** -0.5)) for c in cols]
    lmat2 = lmat_ref[...]
    krow, qidx = _key_query_iota(blk)
    past = krow < qidx

    def k_block(j, c):
        return k_ref[0, pl.ds(pl.multiple_of(j * blk, blk), blk), c]

    def log_terms(k_rows, nq):
        n = lax.dot_general(k_rows, nq, NT_DIMS, preferred_element_type=F32)
        sp = jnp.log(1.0 + jnp.exp(-jnp.abs(n)))
        log_keep = jnp.minimum(n, 0.0) - sp
        return log_keep, log_keep - n

    def later_keys(log_keep):
        hi, lo = _split_bf16(log_keep)
        return jnp.dot(lmat2, jnp.concatenate([hi, lo], axis=0), preferred_element_type=F32)

    def live(carries):
        return (jnp.max(functools.reduce(jnp.maximum, carries)) > -SB_STOP).astype(jnp.int32)

    jp = jnp.maximum(i - 1, 0)
    has_prev = jnp.where(i > 0, 0.0, -jnp.inf)
    terms = [log_terms(jnp.concatenate([k_block(jp, c), k_block(i, c)], axis=0), nq)
             for c, nq in zip(cols, nq2)]
    split = [(jnp.where(past, lk[blk:], 0.0), lk[:blk], lb) for lk, lb in terms]
    laters = [(later_keys(lk_own), later_keys(lk_prev)) for lk_own, lk_prev, _ in split]
    probs, carries = [], []
    for (lk_own, lk_prev, lb), (lt_own, lt_prev) in zip(split, laters):
        c_own = _colsum(lk_own)
        a_own = jnp.where(past, jnp.exp(lb[blk:] + lt_own), 0.0)
        a_prev = jnp.exp(lb[:blk] + lt_prev + (c_own + has_prev))
        probs.append(jnp.concatenate([a_prev, a_own], axis=0).astype(BF16))
        carries.append(c_own + _colsum(lk_prev))
    accs = [jnp.dot(jnp.concatenate([vt_ref[0, jp, c, :], vt_ref[0, i, c, :]], axis=1), p,
                    preferred_element_type=F32) for c, p in zip(cols, probs)]

    def cond(st):
        return jnp.logical_and(st[0] >= 0, st[1] > 0)

    def body(st):
        j, carries, accs = st[0], st[2:2 + ncb], st[2 + ncb:]
        terms = [log_terms(k_block(j, c), nq) for c, nq in zip(cols, nq2)]
        laters = [later_keys(lk) for lk, _ in terms]
        probs = [jnp.exp(lb + lt + cr).astype(BF16) for (_, lb), lt, cr in zip(terms, laters, carries)]
        accs = [acc + jnp.dot(vt_ref[0, j, c, :], p, preferred_element_type=F32)
                for acc, c, p in zip(accs, cols, probs)]
        carries = [cr + _colsum(lk) for cr, (lk, _) in zip(carries, terms)]
        return (j - 1, live(carries), *carries, *accs)

    st = lax.while_loop(cond, body, (i - 2, live(carries), *carries, *accs))
    for cb, c in enumerate(cols):
        o_a, o_b = _own_heads(st[2 + ncb + cb], blk)
        o_ref[0, :, c] = jnp.concatenate([o_a, o_b], axis=0).T.astype(o_ref.dtype)


def _sb_attention(qk, vt, blk):
    b, s, _ = qk.shape
    nb = s // blk
    ncb = SB_COL_BLOCKS
    row = np.arange(blk)
    lmat = np.asarray(row[None, :] > row[:, None], np.float32)
    lmat2 = jnp.asarray(np.concatenate([lmat, lmat], axis=1), BF16)
    return pl.pallas_call(
        functools.partial(_sb_attn_kernel, blk=blk, ncb=ncb),
        out_shape=jax.ShapeDtypeStruct((b, s, D_MODEL), BF16),
        grid=(b, N_QBLK // ncb, nb),
        in_specs=[
            pl.BlockSpec((1, blk, ncb * LANES), lambda bi, c, i: (bi, i, c)),
            pl.BlockSpec((1, s, ncb * LANES), lambda bi, c, i: (bi, 0, N_QBLK // ncb + c)),
            pl.BlockSpec((1, nb, ncb * LANES, blk), lambda bi, c, i: (bi, 0, c, 0)),
            _resident((blk, 2 * blk)),
        ],
        out_specs=pl.BlockSpec((1, blk, ncb * LANES), lambda bi, c, i: (bi, i, c)),
        compiler_params=_compiler_params(3),
        name="sb_attention",
    )(qk, qk, vt, lmat2)


def _moba_attn_kernel(q_ref, k_ref, vt_ref, avg_ref, o_ref, kmean_ref, *, blk, kb, ncb):
    i = pl.program_id(2)
    nb = vt_ref.shape[1]
    nbp = avg_ref.shape[0]
    cols = [slice(cb * LANES, (cb + 1) * LANES) for cb in range(ncb)]

    @pl.when(i == 0)
    def _():
        kmean_ref[...] = jnp.dot(avg_ref[...], k_ref[0], preferred_element_type=F32)

    q2 = [_stack_heads(q_ref[0, :, c], interleaved=True) for c in cols]
    krow, qidx = _key_query_iota(blk)
    causal = krow <= qidx
    nidx = lax.broadcasted_iota(jnp.int32, (nbp, 2 * blk), 0)
    neg_inf = jnp.float32(-jnp.inf)

    picks = []
    for c, q in zip(cols, q2):
        km = jnp.concatenate(_split_bf16(kmean_ref[:, c]), axis=1)
        gate = lax.dot_general(km, jnp.concatenate([q, q], axis=1), NT_DIMS,
                               preferred_element_type=F32)
        gate = jnp.where(nidx < i, gate, neg_inf)
        chosen = []
        for _ in range(MOBA_TOPK):
            top = jnp.max(gate, axis=0, keepdims=True)
            idx = jnp.min(jnp.where(gate == top, nidx, nbp), axis=0, keepdims=True)
            chosen.append(jnp.where(top > neg_inf, idx, -1))
            gate = jnp.where(nidx == idx, neg_inf, gate)
        picks.append(chosen)

    def step(ids, own_first, st):
        load_ids = [jnp.minimum(bid, nb - 1) for bid in ids]
        scores = []
        for cb, c in enumerate(cols):
            k_rows = jnp.concatenate(
                [k_ref[0, pl.ds(pl.multiple_of(bid * blk, blk), blk), c] for bid in load_ids], axis=0)
            s = lax.dot_general(k_rows, q2[cb], NT_DIMS, preferred_element_type=F32)
            parts = []
            for g, bid in enumerate(ids):
                if own_first and g == 0:
                    mask = causal
                else:
                    mask = functools.reduce(jnp.logical_or, [p == bid for p in picks[cb]])
                parts.append(jnp.where(mask, s[g * blk:(g + 1) * blk], neg_inf))
            scores.append(jnp.concatenate(parts, axis=0))
        soft = []
        for cb, s in enumerate(scores):
            m, l = st[4 * cb], st[4 * cb + 1]
            m_new = jnp.maximum(m, jnp.max(s, axis=0, keepdims=True))
            p = jnp.exp(s - m_new)
            alpha = jnp.exp(m - m_new)
            soft.append((m_new, alpha * l + _colsum(p), alpha, p.astype(BF16)))
        out = []
        for cb, (c, (m_new, l_new, alpha, p)) in enumerate(zip(cols, soft)):
            v_cat = jnp.concatenate([vt_ref[0, bid, c, :] for bid in load_ids], axis=1)
            o_a, o_b = _own_heads(jnp.dot(v_cat, p, preferred_element_type=F32), blk)
            out += [m_new, l_new,
                    alpha[:, :blk] * st[4 * cb + 2] + o_a, alpha[:, blk:] * st[4 * cb + 3] + o_b]
        return tuple(out)

    st = (jnp.full((1, 2 * blk), neg_inf, F32), jnp.zeros((1, 2 * blk), F32),
          jnp.zeros((HEAD_DIM, blk), F32), jnp.zeros((HEAD_DIM, blk), F32)) * ncb
    st = step([i] + list(range(kb - 1)), True, st)
    n_rest = jnp.maximum(i - (kb - 1), 0)

    def body(t, st):
        base = (kb - 1) + t * kb
        return step([base + g for g in range(kb)], False, st)

    st = lax.fori_loop(0, (n_rest + kb - 1) // kb, body, st)
    for cb, c in enumerate(cols):
        l, acc_a, acc_b = st[4 * cb + 1:4 * cb + 4]
        o_ref[0, :, c] = jnp.concatenate([acc_a / l[:, :blk], acc_b / l[:, blk:]],
                                         axis=0).T.astype(o_ref.dtype)


def _moba_attention(qk, vt):
    b, s, _ = qk.shape
    blk = MOBA_BLOCK
    nb = s // blk
    nbp = -(-nb // 8) * 8
    ncb = MOBA_COL_BLOCKS
    blk_of_key = np.arange(s) // blk
    avg = jnp.asarray((np.arange(nbp)[:, None] == blk_of_key[None, :]) * (1.0 / blk), BF16)
    return pl.pallas_call(
        functools.partial(_moba_attn_kernel, blk=blk, kb=min(MOBA_TILE_BLOCKS, nb), ncb=ncb),
        out_shape=jax.ShapeDtypeStruct((b, s, D_MODEL), BF16),
        grid=(b, N_QBLK // ncb, nb),
        in_specs=[
            pl.BlockSpec((1, blk, ncb * LANES), lambda bi, c, i: (bi, i, c)),
            pl.BlockSpec((1, s, ncb * LANES), lambda bi, c, i: (bi, 0, N_QBLK // ncb + c)),
            pl.BlockSpec((1, nb, ncb * LANES, blk), lambda bi, c, i: (bi, 0, c, 0)),
            _resident((nbp, s)),
        ],
        out_specs=pl.BlockSpec((1, blk, ncb * LANES), lambda bi, c, i: (bi, i, c)),
        scratch_shapes=[pltpu.VMEM((nbp, ncb * LANES), F32)],
        compiler_params=_compiler_params(3),
        name="moba_attention",
    )(qk, qk, vt, avg)


def _swa_attn_kernel(sink_ref, q_ref, k_ref, vt_ref, o_ref, *, blk, nsub):
    c = pl.program_id(1)
    i = pl.program_id(2)
    krow, qidx = _key_query_iota(blk)
    in_prev = krow > qidx
    in_own = krow <= qidx
    head_b = lax.broadcasted_iota(jnp.int32, (1, 2 * blk), 1) >= blk
    sink = jnp.where(head_b, sink_ref[c + N_QBLK], sink_ref[c])
    neg_inf = jnp.float32(-jnp.inf)

    blocks = [(i * nsub + u, jnp.maximum(i * nsub + u - 1, 0)) for u in range(nsub)]
    scores = []
    for u, (g, gp) in enumerate(blocks):
        q2 = _stack_heads(q_ref[0, u * blk:(u + 1) * blk, :], interleaved=True)
        k_rows = jnp.concatenate(
            [k_ref[0, pl.ds(pl.multiple_of(gp * blk, blk), blk), :],
             k_ref[0, pl.ds(pl.multiple_of(g * blk, blk), blk), :]], axis=0)
        s = lax.dot_general(k_rows, q2, NT_DIMS, preferred_element_type=F32)
        scores.append(jnp.concatenate(
            [jnp.where(jnp.logical_and(in_prev, g > 0), s[:blk], neg_inf),
             jnp.where(in_own, s[blk:], neg_inf)], axis=0))
    probs = []
    for s in scores:
        m = jnp.maximum(jnp.max(s, axis=0, keepdims=True), sink)
        p = jnp.exp(s - m)
        probs.append((p.astype(BF16), _colsum(p) + jnp.exp(sink - m)))
    for u, ((g, gp), (p, denom)) in enumerate(zip(blocks, probs)):
        v_cat = jnp.concatenate([vt_ref[0, gp], vt_ref[0, g]], axis=1)
        o_a, o_b = _own_heads(jnp.dot(v_cat, p, preferred_element_type=F32), blk)
        _store_heads(o_ref, slice(u * blk, (u + 1) * blk), o_a / denom[:, :blk], o_b / denom[:, blk:])


def _swa_attention(qk, vt, sinks):
    b, s, _ = qk.shape
    blk = SWA_WINDOW
    nb = s // blk
    tq = min(SWA_Q_TILE, s)
    return pl.pallas_call(
        functools.partial(_swa_attn_kernel, blk=blk, nsub=tq // blk),
        out_shape=jax.ShapeDtypeStruct((b, s, D_MODEL), BF16),
        grid=(b, N_QBLK, s // tq),
        in_specs=[
            pl.BlockSpec(memory_space=pltpu.SMEM),
            pl.BlockSpec((1, tq, LANES), lambda bi, c, i: (bi, i, c)),
            pl.BlockSpec((1, s, LANES), lambda bi, c, i: (bi, 0, N_QBLK)),
            pl.BlockSpec((1, nb, LANES, blk), lambda bi, c, i: (bi, 0, 0, 0)),
        ],
        out_specs=pl.BlockSpec((1, tq, LANES), lambda bi, c, i: (bi, i, c)),
        compiler_params=_compiler_params(3),
        name="swa_attention",
    )(sinks, qk, qk, vt)


def _out_ffn_kernel(attn_ref, x_ref, mod_ref, gain_ref, wo_ref, wg_ref, wu_ref, wd_ref, o_ref):
    y = jnp.dot(attn_ref[0], wo_ref[...], preferred_element_type=F32)
    x1 = x_ref[0] + mod_ref[0, 0, 2:3, :] * y
    h = _norm_modulate(x1, gain_ref[...], mod_ref[0, 0, 3:4, :], mod_ref[0, 0, 4:5, :])
    hb = h.astype(BF16)
    gate = jnp.dot(hb, wg_ref[...], preferred_element_type=F32)
    up = jnp.dot(hb, wu_ref[...], preferred_element_type=F32)
    act = (gate * (1.0 / (1.0 + jnp.exp(-gate)))) * up
    ff = jnp.dot(act.astype(BF16), wd_ref[...], preferred_element_type=F32)
    o_ref[0] = x1 + mod_ref[0, 0, 5:6, :] * ff


def _out_ffn(attn, x, mod, layer, gain, wo, wg, wu, wd):
    b, s, d = x.shape
    dff = wg.shape[1]
    tm = min(FFN_ROW_TILE, s)
    return pl.pallas_call(
        _out_ffn_kernel,
        out_shape=jax.ShapeDtypeStruct((b, s, d), F32),
        grid=(b, s // tm),
        in_specs=[
            pl.BlockSpec((1, tm, d), lambda bi, i: (bi, i, 0)),
            pl.BlockSpec((1, tm, d), lambda bi, i: (bi, i, 0)),
            pl.BlockSpec((1, 1, 6, d), lambda bi, i: (layer, bi, 0, 0)),
            _resident((1, d)),
            _resident((d, d)),
            _resident((d, dff)),
            _resident((d, dff)),
            _resident((dff, d)),
        ],
        out_specs=pl.BlockSpec((1, tm, d), lambda bi, i: (bi, i, 0)),
        compiler_params=_compiler_params(2),
        name=f"out_ffn_l{layer}",
    )(attn, x, mod, gain.reshape(1, d), wo, wg, wu, wd)


def _interleave_perm(head_a, head_b):
    half = HEAD_DIM // 2
    r = np.arange(half)
    return np.concatenate([head_a * HEAD_DIM + r, head_b * HEAD_DIM + r,
                           head_a * HEAD_DIM + half + r, head_b * HEAD_DIM + half + r])


def _rope_tables(seq_len):
    inv_freq = 1.0 / (ROPE_THETA ** (jnp.arange(0, HEAD_DIM, 2, dtype=F32) / HEAD_DIM))
    ang = jnp.arange(seq_len, dtype=F32)[:, None] * inv_freq[None, :]
    cos, sin = jnp.cos(ang), jnp.sin(ang)
    return (jnp.concatenate([cos] * 4, axis=1),
            jnp.concatenate([-sin, -sin, sin, sin], axis=1))


def _head_sum_matrix():
    half = HEAD_DIM // 2
    head = (np.arange(LANES) // half) % 2
    g = (head[:, None] == head[None, :]).astype(np.float32)
    return jnp.asarray(np.concatenate([g, g], axis=0), BF16)


def _block_gain(qk_gain):
    half = HEAD_DIM // 2
    lane = np.arange(LANES)
    dim = (lane % half) + half * (lane // (2 * half))
    return qk_gain[:, dim]


def kernel(x, c, ada_w, ada_b, norm_gain, ffn_w_gate, ffn_w_up, ffn_w_down, sb_w_in, sb_w_out, moba_w_in, moba_qk_gain, moba_w_out, swa_w_in, swa_qk_gain, swa_sinks, swa_w_out):
    b, s, d = x.shape
    depth = ada_w.shape[0]
    hd_all = N_HEADS * HEAD_DIM
    mod = _ada_modulation(c, ada_w, ada_b).reshape(depth, b, 6, d)
    cos_t, sin_t = _rope_tables(s)
    gsum = _head_sum_matrix()

    moba_cols = np.concatenate(
        [_interleave_perm(2 * cb, 2 * cb + 1) for cb in range(N_QBLK)]
        + [hd_all + _interleave_perm(2 * cb, 2 * cb + 1) for cb in range(N_QBLK)])
    swa_cols = np.concatenate(
        [_interleave_perm(cb, N_QBLK + cb) for cb in range(N_QBLK)]
        + [hd_all + _interleave_perm(0, 1)])
    swa_out_rows = np.concatenate(
        [np.concatenate([np.arange(HEAD_DIM) + cb * HEAD_DIM,
                         np.arange(HEAD_DIM) + (N_QBLK + cb) * HEAD_DIM]) for cb in range(N_QBLK)])

    for i in range(depth):
        kind, j = i % N_MIXERS, i // N_MIXERS
        if kind == 0:
            w_in = sb_w_in[j]
            wqk = w_in[:, :2 * hd_all].astype(BF16)
            wvt = w_in[:, 2 * hd_all:].T.astype(BF16)
            qk, vt = _norm_proj(x, mod, i, norm_gain[i, 0], wqk, wvt, SB_BLOCK)
            attn = _sb_attention(qk, vt, SB_BLOCK)
            wo = sb_w_out[j].astype(BF16)
        elif kind == 1:
            w_in = moba_w_in[j]
            wqk = w_in[:, moba_cols].astype(BF16)
            wvt = w_in[:, 2 * hd_all:].T.astype(BF16)
            rope = (cos_t, sin_t, gsum, _block_gain(moba_qk_gain[j]))
            qk, vt = _norm_proj(x, mod, i, norm_gain[i, 0], wqk, wvt, MOBA_BLOCK, rope)
            attn = _moba_attention(qk, vt)
            wo = moba_w_out[j].astype(BF16)
        else:
            w_in = swa_w_in[j]
            wqk = w_in[:, swa_cols].astype(BF16)
            wvt = w_in[:, hd_all + SWA_KV_HEADS * HEAD_DIM:].T.astype(BF16)
            rope = (cos_t, sin_t, gsum, _block_gain(swa_qk_gain[j]))
            qk, vt = _norm_proj(x, mod, i, norm_gain[i, 0], wqk, wvt, SWA_WINDOW, rope)
            attn = _swa_attention(qk, vt, swa_sinks[j])
            wo = swa_w_out[j][swa_out_rows, :].astype(BF16)
        x = _out_ffn(attn, x, mod, i, norm_gain[i, 1], wo,
                     ffn_w_gate[i].astype(BF16), ffn_w_up[i].astype(BF16), ffn_w_down[i].astype(BF16))
    return x
```

```python
import functools

import numpy as np
import jax
import jax.numpy as jnp
from jax import lax
from jax.experimental import pallas as pl
from jax.experimental.pallas import tpu as pltpu

D_MODEL = 1024
HEAD_DIM = 64
N_HEADS = D_MODEL // HEAD_DIM
SWA_KV_HEADS = 2
N_MIXERS = 3
MOBA_BLOCK = 256
MOBA_TOPK = 3
SWA_WINDOW = 128
ROPE_THETA = 10000.0
EPS = 1e-6

LANES = 128
N_QBLK = D_MODEL // LANES
VMEM_LIMIT_BYTES = 56 * 1024 * 1024
SB_BLOCK = 256
SB_STOP = 110.0
SB_COL_BLOCKS = 4
MOBA_TILE_BLOCKS = 2
MOBA_COL_BLOCKS = 4
SWA_Q_TILE = 1024
ROW_TILE = 512
FFN_ROW_TILE = 512
FFN_CHUNKS = 2
MXU_WIDTH = 256

LOG2E = 1.4426950408889634
SOFTMAX_SCALE_LOG2 = HEAD_DIM ** -0.5 * LOG2E

BF16 = jnp.bfloat16
F32 = jnp.float32
NT_DIMS = (((1,), (1,)), ((), ()))


def _compiler_params(n_grid):
    return pltpu.CompilerParams(
        dimension_semantics=("arbitrary",) * n_grid,
        vmem_limit_bytes=VMEM_LIMIT_BYTES)


def _resident(shape):
    nd = len(shape)
    return pl.BlockSpec(shape, lambda *_: (0,) * nd, pipeline_mode=pl.Buffered(1))


def _split_bf16(a):
    hi = a.astype(BF16)
    lo = (a - hi.astype(F32)).astype(BF16)
    return hi, lo


def _ada_kernel(c_ref, w_ref, b_ref, o_ref):
    c = c_ref[...]
    c_act = c * (1.0 / (1.0 + jnp.exp(-c)))
    o_ref[0] = jnp.dot(c_act, w_ref[0], preferred_element_type=F32,
                       precision=lax.Precision.HIGHEST) + b_ref[0]


def _ada_modulation(c, ada_w, ada_b):
    depth, d, n = ada_w.shape
    b = c.shape[0]
    tn = n // 4
    return pl.pallas_call(
        _ada_kernel,
        out_shape=jax.ShapeDtypeStruct((depth, b, n), F32),
        grid=(depth, n // tn),
        in_specs=[
            pl.BlockSpec((b, d), lambda l, j: (0, 0)),
            pl.BlockSpec((1, d, tn), lambda l, j: (l, 0, j)),
            pl.BlockSpec((1, 1, tn), lambda l, j: (l, 0, j)),
        ],
        out_specs=pl.BlockSpec((1, b, tn), lambda l, j: (l, 0, j)),
        compiler_params=_compiler_params(2),
        name="ada_modulation",
    )(c, ada_w, ada_b.reshape(depth, 1, n))


def _norm_modulate(x, gain, shift, scale):
    var = jnp.mean(x * x, axis=-1, keepdims=True)
    h = (x * lax.rsqrt(var + EPS)) * gain
    return h * (1.0 + scale) + shift


def _norm_proj_kernel(*refs, n_rope, tk):
    if n_rope:
        (x_ref, mod_ref, gain_ref, wqk_ref, wvt_ref,
         cos_ref, sin_ref, gsum_ref, qkg_ref, oqk_ref, ovt_ref) = refs
    else:
        x_ref, mod_ref, gain_ref, wqk_ref, wvt_ref, oqk_ref, ovt_ref = refs
    x = x_ref[0]
    h = _norm_modulate(x, gain_ref[...], mod_ref[0, 0, 0:1, :], mod_ref[0, 0, 1:2, :])
    hb = h.astype(BF16)
    vt = lax.dot_general(wvt_ref[...], hb, NT_DIMS, preferred_element_type=F32)
    tm = x.shape[0]
    for t in range(tm // tk):
        ovt_ref[0, t] = vt[:, t * tk:(t + 1) * tk].astype(BF16)
    qk = jnp.dot(hb, wqk_ref[...], preferred_element_type=F32)
    if not n_rope:
        oqk_ref[0] = qk.astype(BF16)
        return
    cos = cos_ref[...]
    sin = sin_ref[...]
    gsum = gsum_ref[...]
    for cb in range(n_rope):
        a = qk[:, cb * LANES:(cb + 1) * LANES]
        hi, lo = _split_bf16(a * a)
        ss = jnp.dot(jnp.concatenate([hi, lo], axis=1), gsum, preferred_element_type=F32)
        r = lax.rsqrt(ss * (1.0 / HEAD_DIM) + EPS)
        g = qkg_ref[0:1, :] if cb < N_QBLK else qkg_ref[1:2, :]
        xn = (a * r) * g
        out = xn * cos + pltpu.roll(xn, LANES // 2, 1) * sin
        oqk_ref[0, :, cb * LANES:(cb + 1) * LANES] = out.astype(BF16)


def _norm_proj(x, mod, layer, gain, wqk, wvt, tk, rope=None):
    b, s, d = x.shape
    nqk = wqk.shape[1]
    nv = wvt.shape[0]
    tm = min(ROW_TILE, s)
    n_rope = 0 if rope is None else nqk // LANES
    in_specs = [
        pl.BlockSpec((1, tm, d), lambda bi, i: (bi, i, 0)),
        pl.BlockSpec((1, 1, 6, d), lambda bi, i: (layer, bi, 0, 0)),
        _resident((1, d)),
        _resident((d, nqk)),
        _resident((nv, d)),
    ]
    args = [x, mod, gain.reshape(1, d), wqk, wvt]
    if rope is not None:
        cos_t, sin_t, gsum, qkg = rope
        in_specs += [
            pl.BlockSpec((tm, LANES), lambda bi, i: (i, 0)),
            pl.BlockSpec((tm, LANES), lambda bi, i: (i, 0)),
            _resident((2 * LANES, LANES)),
            _resident((2, LANES)),
        ]
        args += [cos_t, sin_t, gsum, qkg]
    return pl.pallas_call(
        functools.partial(_norm_proj_kernel, n_rope=n_rope, tk=tk),
        out_shape=(jax.ShapeDtypeStruct((b, s, nqk), BF16),
                   jax.ShapeDtypeStruct((b, s // tk, nv, tk), BF16)),
        grid=(b, s // tm),
        in_specs=in_specs,
        out_specs=(pl.BlockSpec((1, tm, nqk), lambda bi, i: (bi, i, 0)),
                   pl.BlockSpec((1, tm // tk, nv, tk), lambda bi, i: (bi, i, 0, 0))),
        compiler_params=_compiler_params(2),
        name=f"norm_proj_l{layer}",
    )(*args)


def _stack_heads(q, interleaved, scale=HEAD_DIM ** -0.5):
    q = (q.astype(F32) * scale).astype(BF16)
    lane = lax.broadcasted_iota(jnp.int32, q.shape, 1)
    first = ((lane // (HEAD_DIM // 2)) % 2 == 0) if interleaved else (lane < HEAD_DIM)
    zero = jnp.zeros_like(q)
    return jnp.concatenate([jnp.where(first, q, zero), jnp.where(first, zero, q)], axis=0)


def _own_heads(o, tq):
    return o[:HEAD_DIM, :tq], o[HEAD_DIM:, tq:]


def _store_heads(o_ref, rows, o_a, o_b):
    o_ref[0, rows, :] = jnp.concatenate([o_a, o_b], axis=0).T.astype(o_ref.dtype)


def _key_query_iota(blk):
    krow = lax.broadcasted_iota(jnp.int32, (blk, 2 * blk), 0)
    qcol = lax.broadcasted_iota(jnp.int32, (blk, 2 * blk), 1)
    return krow, jnp.where(qcol >= blk, qcol - blk, qcol)


def _colsum(a):
    return jnp.sum(a, axis=0, keepdims=True)


def _sb_attn_kernel(q_ref, k_ref, vt_ref, lmat_ref, o_ref, *, blk, ncb):
    i = pl.program_id(2)
    cols = [slice(cb * LANES, (cb + 1) * LANES) for cb in range(ncb)]
    nq2 = [_stack_heads(q_ref[0, :, c], interleaved=False, scale=-SOFTMAX_SCALE_LOG2) for c in cols]
    lmat = lmat_ref[...]
    krow, qidx = _key_query_iota(blk)
    past = krow < qidx

    def k_block(j, c):
        return k_ref[0, pl.ds(pl.multiple_of(j * blk, blk), blk), c]

    def log_terms(k_rows, nq):
        n = lax.dot_general(k_rows, nq, NT_DIMS, preferred_element_type=F32)
        sp = jnp.log(1.0 + jnp.exp2(-jnp.abs(n))) * LOG2E
        log_keep = jnp.minimum(n, 0.0) - sp
        return log_keep, log_keep - n

    def later_keys(log_keep):
        return jnp.dot(lmat, log_keep.astype(BF16), preferred_element_type=F32)

    def live(carries):
        return (jnp.max(functools.reduce(jnp.maximum, carries)) > -SB_STOP * LOG2E).astype(jnp.int32)

    jp = jnp.maximum(i - 1, 0)
    has_prev = jnp.where(i > 0, 0.0, -jnp.inf)
    terms = [log_terms(jnp.concatenate([k_block(jp, c), k_block(i, c)], axis=0), nq)
             for c, nq in zip(cols, nq2)]
    split = [(jnp.where(past, lk[blk:], 0.0), lk[:blk], lb) for lk, lb in terms]
    laters = [(later_keys(lk_own), later_keys(lk_prev)) for lk_own, lk_prev, _ in split]
    probs, carries = [], []
    for (lk_own, lk_prev, lb), (lt_own, lt_prev) in zip(split, laters):
        c_own = _colsum(lk_own)
        a_own = jnp.where(past, jnp.exp2(lb[blk:] + lt_own), 0.0)
        a_prev = jnp.exp2(lb[:blk] + lt_prev + (c_own + has_prev))
        probs.append(jnp.concatenate([a_prev, a_own], axis=0).astype(BF16))
        carries.append(c_own + _colsum(lk_prev))
    accs = [jnp.dot(jnp.concatenate([vt_ref[0, jp, c, :], vt_ref[0, i, c, :]], axis=1), p,
                    preferred_element_type=F32) for c, p in zip(cols, probs)]

    def cond(st):
        return jnp.logical_and(st[0] >= 0, st[1] > 0)

    def body(st):
        j, carries, accs = st[0], st[2:2 + ncb], st[2 + ncb:]
        terms = [log_terms(k_block(j, c), nq) for c, nq in zip(cols, nq2)]
        laters = [later_keys(lk) for lk, _ in terms]
        probs = [jnp.exp2(lb + lt + cr).astype(BF16) for (_, lb), lt, cr in zip(terms, laters, carries)]
        accs = [acc + jnp.dot(vt_ref[0, j, c, :], p, preferred_element_type=F32)
                for acc, c, p in zip(accs, cols, probs)]
        carries = [cr + _colsum(lk) for cr, (lk, _) in zip(carries, terms)]
        return (j - 1, live(carries), *carries, *accs)

    st = lax.while_loop(cond, body, (i - 2, live(carries), *carries, *accs))
    for cb, c in enumerate(cols):
        o_a, o_b = _own_heads(st[2 + ncb + cb], blk)
        o_ref[0, :, c] = jnp.concatenate([o_a, o_b], axis=0).T.astype(o_ref.dtype)


def _sb_attention(qk, vt, blk):
    b, s, _ = qk.shape
    nb = s // blk
    ncb = SB_COL_BLOCKS
    row = np.arange(blk)
    lmat = jnp.asarray(row[None, :] > row[:, None], BF16)
    return pl.pallas_call(
        functools.partial(_sb_attn_kernel, blk=blk, ncb=ncb),
        out_shape=jax.ShapeDtypeStruct((b, s, D_MODEL), BF16),
        grid=(b, N_QBLK // ncb, nb),
        in_specs=[
            pl.BlockSpec((1, blk, ncb * LANES), lambda bi, c, i: (bi, i, c)),
            pl.BlockSpec((1, s, ncb * LANES), lambda bi, c, i: (bi, 0, N_QBLK // ncb + c)),
            pl.BlockSpec((1, nb, ncb * LANES, blk), lambda bi, c, i: (bi, 0, c, 0)),
            _resident((blk, blk)),
        ],
        out_specs=pl.BlockSpec((1, blk, ncb * LANES), lambda bi, c, i: (bi, i, c)),
        compiler_params=_compiler_params(3),
        name="sb_attention",
    )(qk, qk, vt, lmat)


def _moba_attn_kernel(q_ref, k_ref, vt_ref, avg_ref, o_ref, kmean_ref, *, blk, kb, ncb):
    i = pl.program_id(2)
    nb = vt_ref.shape[1]
    nbp = avg_ref.shape[0]
    cols = [slice(cb * LANES, (cb + 1) * LANES) for cb in range(ncb)]

    @pl.when(i == 0)
    def _():
        kmean_ref[...] = jnp.dot(avg_ref[...], k_ref[0], preferred_element_type=F32)

    q2 = [_stack_heads(q_ref[0, :, c], interleaved=True, scale=SOFTMAX_SCALE_LOG2) for c in cols]
    krow, qidx = _key_query_iota(blk)
    causal = krow <= qidx
    nidx = lax.broadcasted_iota(jnp.int32, (nbp, 2 * blk), 0)
    neg_inf = jnp.float32(-jnp.inf)

    picks = []
    for c, q in zip(cols, q2):
        km = jnp.concatenate(_split_bf16(kmean_ref[:, c]), axis=1)
        gate = lax.dot_general(km, jnp.concatenate([q, q], axis=1), NT_DIMS,
                               preferred_element_type=F32)
        gate = jnp.where(nidx < i, gate, neg_inf)
        chosen = []
        for _ in range(MOBA_TOPK):
            top = jnp.max(gate, axis=0, keepdims=True)
            idx = jnp.min(jnp.where(gate == top, nidx, nbp), axis=0, keepdims=True)
            chosen.append(jnp.where(top > neg_inf, idx, -1))
            gate = jnp.where(nidx == idx, neg_inf, gate)
        picks.append(chosen)

    def score_stage(ids, own_first):
        load_ids = [jnp.minimum(bid, nb - 1) for bid in ids]
        scores = []
        for cb, c in enumerate(cols):
            k_rows = jnp.concatenate(
                [k_ref[0, pl.ds(pl.multiple_of(bid * blk, blk), blk), c] for bid in load_ids], axis=0)
            s = lax.dot_general(k_rows, q2[cb], NT_DIMS, preferred_element_type=F32)
            parts = []
            for g, bid in enumerate(ids):
                if own_first and g == 0:
                    mask = causal
                else:
                    mask = functools.reduce(jnp.logical_or, [p == bid for p in picks[cb]])
                parts.append(jnp.where(mask, s[g * blk:(g + 1) * blk], neg_inf))
            scores.append(jnp.concatenate(parts, axis=0))
        return load_ids, scores

    def softmax_stage(scores, st):
        soft = []
        for cb, s in enumerate(scores):
            m, l = st[4 * cb], st[4 * cb + 1]
            m_new = jnp.maximum(m, jnp.max(s, axis=0, keepdims=True))
            p = jnp.exp2(s - m_new)
            alpha = jnp.exp2(m - m_new)
            soft.append((m_new, alpha * l + _colsum(p), alpha, p.astype(BF16)))
        return soft

    def value_stage(load_ids, soft, st):
        out = []
        for cb, (c, (m_new, l_new, alpha, p)) in enumerate(zip(cols, soft)):
            v_cat = jnp.concatenate([vt_ref[0, bid, c, :] for bid in load_ids], axis=1)
            o_a, o_b = _own_heads(jnp.dot(v_cat, p, preferred_element_type=F32), blk)
            out += [m_new, l_new,
                    alpha[:, :blk] * st[4 * cb + 2] + o_a, alpha[:, blk:] * st[4 * cb + 3] + o_b]
        return tuple(out)

    def step(ids, own_first, st):
        load_ids, scores = score_stage(ids, own_first)
        return value_stage(load_ids, softmax_stage(scores, st), st)

    def past_ids(t):
        return [(kb - 1) + t * kb + g for g in range(kb)]

    st = (jnp.full((1, 2 * blk), neg_inf, F32), jnp.zeros((1, 2 * blk), F32),
          jnp.zeros((HEAD_DIM, blk), F32), jnp.zeros((HEAD_DIM, blk), F32)) * ncb
    st = step([i] + list(range(kb - 1)), True, st)
    n_steps = (jnp.maximum(i - (kb - 1), 0) + kb - 1) // kb

    def two_steps(t2, st):
        ids_a, scores_a = score_stage(past_ids(2 * t2), False)
        soft_a = softmax_stage(scores_a, st)
        ids_b, scores_b = score_stage(past_ids(2 * t2 + 1), False)
        st = value_stage(ids_a, soft_a, st)
        return value_stage(ids_b, softmax_stage(scores_b, st), st)

    st = lax.fori_loop(0, n_steps // 2, two_steps, st)
    st = lax.fori_loop(0, n_steps % 2, lambda _, st: step(past_ids(n_steps - 1), False, st), st)
    for cb, c in enumerate(cols):
        l, acc_a, acc_b = st[4 * cb + 1:4 * cb + 4]
        o_ref[0, :, c] = jnp.concatenate([acc_a / l[:, :blk], acc_b / l[:, blk:]],
                                         axis=0).T.astype(o_ref.dtype)


def _moba_attention(qk, vt):
    b, s, _ = qk.shape
    blk = MOBA_BLOCK
    nb = s // blk
    nbp = -(-nb // 8) * 8
    ncb = MOBA_COL_BLOCKS
    blk_of_key = np.arange(s) // blk
    avg = jnp.asarray((np.arange(nbp)[:, None] == blk_of_key[None, :]) * (1.0 / blk), BF16)
    return pl.pallas_call(
        functools.partial(_moba_attn_kernel, blk=blk, kb=min(MOBA_TILE_BLOCKS, nb), ncb=ncb),
        out_shape=jax.ShapeDtypeStruct((b, s, D_MODEL), BF16),
        grid=(b, N_QBLK // ncb, nb),
        in_specs=[
            pl.BlockSpec((1, blk, ncb * LANES), lambda bi, c, i: (bi, i, c)),
            pl.BlockSpec((1, s, ncb * LANES), lambda bi, c, i: (bi, 0, N_QBLK // ncb + c)),
            pl.BlockSpec((1, nb, ncb * LANES, blk), lambda bi, c, i: (bi, 0, c, 0)),
            _resident((nbp, s)),
        ],
        out_specs=pl.BlockSpec((1, blk, ncb * LANES), lambda bi, c, i: (bi, i, c)),
        scratch_shapes=[pltpu.VMEM((nbp, ncb * LANES), F32)],
        compiler_params=_compiler_params(3),
        name="moba_attention",
    )(qk, qk, vt, avg)


def _swa_attn_kernel(sink_ref, q_ref, k_ref, vt_ref, o_ref, *, blk, nsub):
    c = pl.program_id(1)
    i = pl.program_id(2)
    krow, qidx = _key_query_iota(blk)
    in_prev = krow > qidx
    in_own = krow <= qidx
    head_b = lax.broadcasted_iota(jnp.int32, (1, 2 * blk), 1) >= blk
    sink = jnp.where(head_b, sink_ref[c + N_QBLK], sink_ref[c]) * LOG2E
    neg_inf = jnp.float32(-jnp.inf)

    blocks = [(i * nsub + u, jnp.maximum(i * nsub + u - 1, 0)) for u in range(nsub)]
    scores = []
    for u, (g, gp) in enumerate(blocks):
        q2 = _stack_heads(q_ref[0, u * blk:(u + 1) * blk, :], interleaved=True, scale=SOFTMAX_SCALE_LOG2)
        k_rows = jnp.concatenate(
            [k_ref[0, pl.ds(pl.multiple_of(gp * blk, blk), blk), :],
             k_ref[0, pl.ds(pl.multiple_of(g * blk, blk), blk), :]], axis=0)
        s = lax.dot_general(k_rows, q2, NT_DIMS, preferred_element_type=F32)
        scores.append(jnp.concatenate(
            [jnp.where(jnp.logical_and(in_prev, g > 0), s[:blk], neg_inf),
             jnp.where(in_own, s[blk:], neg_inf)], axis=0))
    probs = []
    for s in scores:
        m = jnp.maximum(jnp.max(s, axis=0, keepdims=True), sink)
        p = jnp.exp2(s - m)
        probs.append((p.astype(BF16), _colsum(p) + jnp.exp2(sink - m)))
    for u, ((g, gp), (p, denom)) in enumerate(zip(blocks, probs)):
        v_cat = jnp.concatenate([vt_ref[0, gp], vt_ref[0, g]], axis=1)
        o_a, o_b = _own_heads(jnp.dot(v_cat, p, preferred_element_type=F32), blk)
        _store_heads(o_ref, slice(u * blk, (u + 1) * blk), o_a / denom[:, :blk], o_b / denom[:, blk:])


def _swa_attention(qk, vt, sinks):
    b, s, _ = qk.shape
    blk = SWA_WINDOW
    nb = s // blk
    tq = min(SWA_Q_TILE, s)
    return pl.pallas_call(
        functools.partial(_swa_attn_kernel, blk=blk, nsub=tq // blk),
        out_shape=jax.ShapeDtypeStruct((b, s, D_MODEL), BF16),
        grid=(b, N_QBLK, s // tq),
        in_specs=[
            pl.BlockSpec(memory_space=pltpu.SMEM),
            pl.BlockSpec((1, tq, LANES), lambda bi, c, i: (bi, i, c)),
            pl.BlockSpec((1, s, LANES), lambda bi, c, i: (bi, 0, N_QBLK)),
            pl.BlockSpec((1, nb, LANES, blk), lambda bi, c, i: (bi, 0, 0, 0)),
        ],
        out_specs=pl.BlockSpec((1, tq, LANES), lambda bi, c, i: (bi, i, c)),
        compiler_params=_compiler_params(3),
        name="swa_attention",
    )(sinks, qk, qk, vt)


def _out_ffn_kernel(attn_ref, x_ref, mod_ref, gain_ref, wo_ref, wg_ref, wu_ref, wd_ref, o_ref):
    y = jnp.dot(attn_ref[0], wo_ref[...], preferred_element_type=F32)
    x1 = x_ref[0] + mod_ref[0, 0, 2:3, :] * y
    h = _norm_modulate(x1, gain_ref[...], mod_ref[0, 0, 3:4, :], mod_ref[0, 0, 4:5, :])
    hb = h.astype(BF16)
    dff = wg_ref.shape[1]
    n_tiles = dff // MXU_WIDTH
    edges = [MXU_WIDTH * ((n_tiles * ck + FFN_CHUNKS - 1) // FFN_CHUNKS) for ck in range(FFN_CHUNKS)] + [dff]
    ff = None
    for ck in range(FFN_CHUNKS):
        cs = slice(edges[ck], edges[ck + 1])
        gate = jnp.dot(hb, wg_ref[:, cs], preferred_element_type=F32)
        up = jnp.dot(hb, wu_ref[:, cs], preferred_element_type=F32)
        act = (gate * (1.0 / (1.0 + jnp.exp(-gate)))) * up
        part = jnp.dot(act.astype(BF16), wd_ref[cs, :], preferred_element_type=F32)
        ff = part if ff is None else ff + part
    o_ref[0] = x1 + mod_ref[0, 0, 5:6, :] * ff


def _out_ffn(attn, x, mod, layer, gain, wo, wg, wu, wd):
    b, s, d = x.shape
    dff = wg.shape[1]
    tm = min(FFN_ROW_TILE, s)
    return pl.pallas_call(
        _out_ffn_kernel,
        out_shape=jax.ShapeDtypeStruct((b, s, d), F32),
        grid=(b, s // tm),
        in_specs=[
            pl.BlockSpec((1, tm, d), lambda bi, i: (bi, i, 0)),
            pl.BlockSpec((1, tm, d), lambda bi, i: (bi, i, 0)),
            pl.BlockSpec((1, 1, 6, d), lambda bi, i: (layer, bi, 0, 0)),
            _resident((1, d)),
            _resident((d, d)),
            _resident((d, dff)),
            _resident((d, dff)),
            _resident((dff, d)),
        ],
        out_specs=pl.BlockSpec((1, tm, d), lambda bi, i: (bi, i, 0)),
        compiler_params=_compiler_params(2),
        name=f"out_ffn_l{layer}",
    )(attn, x, mod, gain.reshape(1, d), wo, wg, wu, wd)


def _interleave_perm(head_a, head_b):
    half = HEAD_DIM // 2
    r = np.arange(half)
    return np.concatenate([head_a * HEAD_DIM + r, head_b * HEAD_DIM + r,
                           head_a * HEAD_DIM + half + r, head_b * HEAD_DIM + half + r])


def _rope_tables(seq_len):
    inv_freq = 1.0 / (ROPE_THETA ** (jnp.arange(0, HEAD_DIM, 2, dtype=F32) / HEAD_DIM))
    ang = jnp.arange(seq_len, dtype=F32)[:, None] * inv_freq[None, :]
    cos, sin = jnp.cos(ang), jnp.sin(ang)
    return (jnp.concatenate([cos] * 4, axis=1),
            jnp.concatenate([-sin, -sin, sin, sin], axis=1))


def _head_sum_matrix():
    half = HEAD_DIM // 2
    head = (np.arange(LANES) // half) % 2
    g = (head[:, None] == head[None, :]).astype(np.float32)
    return jnp.asarray(np.concatenate([g, g], axis=0), BF16)


def _block_gain(qk_gain):
    half = HEAD_DIM // 2
    lane = np.arange(LANES)
    dim = (lane % half) + half * (lane // (2 * half))
    return qk_gain[:, dim]


def kernel(x, c, ada_w, ada_b, norm_gain, ffn_w_gate, ffn_w_up, ffn_w_down, sb_w_in, sb_w_out, moba_w_in, moba_qk_gain, moba_w_out, swa_w_in, swa_qk_gain, swa_sinks, swa_w_out):
    b, s, d = x.shape
    depth = ada_w.shape[0]
    hd_all = N_HEADS * HEAD_DIM
    mod = _ada_modulation(c, ada_w, ada_b).reshape(depth, b, 6, d)
    cos_t, sin_t = _rope_tables(s)
    gsum = _head_sum_matrix()

    moba_cols = np.concatenate(
        [_interleave_perm(2 * cb, 2 * cb + 1) for cb in range(N_QBLK)]
        + [hd_all + _interleave_perm(2 * cb, 2 * cb + 1) for cb in range(N_QBLK)])
    swa_cols = np.concatenate(
        [_interleave_perm(cb, N_QBLK + cb) for cb in range(N_QBLK)]
        + [hd_all + _interleave_perm(0, 1)])
    swa_out_rows = np.concatenate(
        [np.concatenate([np.arange(HEAD_DIM) + cb * HEAD_DIM,
                         np.arange(HEAD_DIM) + (N_QBLK + cb) * HEAD_DIM]) for cb in range(N_QBLK)])

    for i in range(depth):
        kind, j = i % N_MIXERS, i // N_MIXERS
        if kind == 0:
            w_in = sb_w_in[j]
            wqk = w_in[:, :2 * hd_all].astype(BF16)
            wvt = w_in[:, 2 * hd_all:].T.astype(BF16)
            qk, vt = _norm_proj(x, mod, i, norm_gain[i, 0], wqk, wvt, SB_BLOCK)
            attn = _sb_attention(qk, vt, SB_BLOCK)
            wo = sb_w_out[j].astype(BF16)
        elif kind == 1:
            w_in = moba_w_in[j]
            wqk = w_in[:, moba_cols].astype(BF16)
            wvt = w_in[:, 2 * hd_all:].T.astype(BF16)
            rope = (cos_t, sin_t, gsum, _block_gain(moba_qk_gain[j]))
            qk, vt = _norm_proj(x, mod, i, norm_gain[i, 0], wqk, wvt, MOBA_BLOCK, rope)
            attn = _moba_attention(qk, vt)
            wo = moba_w_out[j].astype(BF16)
        else:
            w_in = swa_w_in[j]
            wqk = w_in[:, swa_cols].astype(BF16)
            wvt = w_in[:, hd_all + SWA_KV_HEADS * HEAD_DIM:].T.astype(BF16)
            rope = (cos_t, sin_t, gsum, _block_gain(swa_qk_gain[j]))
            qk, vt = _norm_proj(x, mod, i, norm_gain[i, 0], wqk, wvt, SWA_WINDOW, rope)
            attn = _swa_attention(qk, vt, swa_sinks[j])
            wo = swa_w_out[j][swa_out_rows, :].astype(BF16)
        x = _out_ffn(attn, x, mod, i, norm_gain[i, 1], wo,
                     ffn_w_gate[i].astype(BF16), ffn_w_up[i].astype(BF16), ffn_w_down[i].astype(BF16))
    return x
```

```python
import functools

import numpy as np
import jax
import jax.numpy as jnp
from jax import lax
from jax.experimental import pallas as pl
from jax.experimental.pallas import tpu as pltpu

D_MODEL = 1024
HEAD_DIM = 64
N_HEADS = D_MODEL // HEAD_DIM
SWA_KV_HEADS = 2
N_MIXERS = 3
MOBA_BLOCK = 256
MOBA_TOPK = 3
SWA_WINDOW = 128
ROPE_THETA = 10000.0
EPS = 1e-6

LANES = 128
N_QBLK = D_MODEL // LANES
VMEM_LIMIT_BYTES = 56 * 1024 * 1024
SB_BLOCK = 128
SB_FIRST_BLOCKS = 3
SB_STOP = 110.0
SB_COL_BLOCKS = 8
MOBA_TILE_BLOCKS = 2
MOBA_COL_BLOCKS = 4
SWA_Q_TILE = 1024
ROW_TILE = 512
FFN_ROW_TILE = 512
FFN_CHUNKS = 2
MXU_WIDTH = 256

LOG2E = 1.4426950408889634
SOFTMAX_SCALE_LOG2 = HEAD_DIM ** -0.5 * LOG2E

BF16 = jnp.bfloat16
F32 = jnp.float32
NT_DIMS = (((1,), (1,)), ((), ()))


def _compiler_params(n_grid):
    return pltpu.CompilerParams(
        dimension_semantics=("arbitrary",) * n_grid,
        vmem_limit_bytes=VMEM_LIMIT_BYTES)


def _resident(shape):
    nd = len(shape)
    return pl.BlockSpec(shape, lambda *_: (0,) * nd, pipeline_mode=pl.Buffered(1))


def _split_bf16(a):
    hi = a.astype(BF16)
    lo = (a - hi.astype(F32)).astype(BF16)
    return hi, lo


def _ada_kernel(c_ref, w_ref, b_ref, o_ref):
    c = c_ref[...]
    c_act = c * (1.0 / (1.0 + jnp.exp(-c)))
    o_ref[0] = jnp.dot(c_act, w_ref[0], preferred_element_type=F32,
                       precision=lax.Precision.HIGHEST) + b_ref[0]


def _ada_modulation(c, ada_w, ada_b):
    depth, d, n = ada_w.shape
    b = c.shape[0]
    tn = n // 4
    return pl.pallas_call(
        _ada_kernel,
        out_shape=jax.ShapeDtypeStruct((depth, b, n), F32),
        grid=(depth, n // tn),
        in_specs=[
            pl.BlockSpec((b, d), lambda l, j: (0, 0)),
            pl.BlockSpec((1, d, tn), lambda l, j: (l, 0, j)),
            pl.BlockSpec((1, 1, tn), lambda l, j: (l, 0, j)),
        ],
        out_specs=pl.BlockSpec((1, b, tn), lambda l, j: (l, 0, j)),
        compiler_params=_compiler_params(2),
        name="ada_modulation",
    )(c, ada_w, ada_b.reshape(depth, 1, n))


def _norm_modulate(x, gain, shift, scale):
    var = jnp.mean(x * x, axis=-1, keepdims=True)
    h = (x * lax.rsqrt(var + EPS)) * gain
    return h * (1.0 + scale) + shift


def _norm_proj_kernel(*refs, n_rope, tk):
    if n_rope:
        (x_ref, mod_ref, gain_ref, wqk_ref, wvt_ref,
         cos_ref, sin_ref, gsum_ref, qkg_ref, oqk_ref, ovt_ref) = refs
    else:
        x_ref, mod_ref, gain_ref, wqk_ref, wvt_ref, oqk_ref, ovt_ref = refs
    x = x_ref[0]
    h = _norm_modulate(x, gain_ref[...], mod_ref[0, 0, 0:1, :], mod_ref[0, 0, 1:2, :])
    hb = h.astype(BF16)
    vt = lax.dot_general(wvt_ref[...], hb, NT_DIMS, preferred_element_type=F32)
    tm = x.shape[0]
    for t in range(tm // tk):
        ovt_ref[0, t] = vt[:, t * tk:(t + 1) * tk].astype(BF16)
    qk = jnp.dot(hb, wqk_ref[...], preferred_element_type=F32)
    if not n_rope:
        oqk_ref[0] = qk.astype(BF16)
        return
    cos = cos_ref[...]
    sin = sin_ref[...]
    gsum = gsum_ref[...]
    for cb in range(n_rope):
        a = qk[:, cb * LANES:(cb + 1) * LANES]
        hi, lo = _split_bf16(a * a)
        ss = jnp.dot(jnp.concatenate([hi, lo], axis=1), gsum, preferred_element_type=F32)
        r = lax.rsqrt(ss * (1.0 / HEAD_DIM) + EPS)
        g = qkg_ref[0:1, :] if cb < N_QBLK else qkg_ref[1:2, :]
        xn = (a * r) * g
        out = xn * cos + pltpu.roll(xn, LANES // 2, 1) * sin
        oqk_ref[0, :, cb * LANES:(cb + 1) * LANES] = out.astype(BF16)


def _norm_proj(x, mod, layer, gain, wqk, wvt, tk, rope=None):
    b, s, d = x.shape
    nqk = wqk.shape[1]
    nv = wvt.shape[0]
    tm = min(ROW_TILE, s)
    n_rope = 0 if rope is None else nqk // LANES
    in_specs = [
        pl.BlockSpec((1, tm, d), lambda bi, i: (bi, i, 0)),
        pl.BlockSpec((1, 1, 6, d), lambda bi, i: (layer, bi, 0, 0)),
        _resident((1, d)),
        _resident((d, nqk)),
        _resident((nv, d)),
    ]
    args = [x, mod, gain.reshape(1, d), wqk, wvt]
    if rope is not None:
        cos_t, sin_t, gsum, qkg = rope
        in_specs += [
            pl.BlockSpec((tm, LANES), lambda bi, i: (i, 0)),
            pl.BlockSpec((tm, LANES), lambda bi, i: (i, 0)),
            _resident((2 * LANES, LANES)),
            _resident((2, LANES)),
        ]
        args += [cos_t, sin_t, gsum, qkg]
    return pl.pallas_call(
        functools.partial(_norm_proj_kernel, n_rope=n_rope, tk=tk),
        out_shape=(jax.ShapeDtypeStruct((b, s, nqk), BF16),
                   jax.ShapeDtypeStruct((b, s // tk, nv, tk), BF16)),
        grid=(b, s // tm),
        in_specs=in_specs,
        out_specs=(pl.BlockSpec((1, tm, nqk), lambda bi, i: (bi, i, 0)),
                   pl.BlockSpec((1, tm // tk, nv, tk), lambda bi, i: (bi, i, 0, 0))),
        compiler_params=_compiler_params(2),
        name=f"norm_proj_l{layer}",
    )(*args)


def _stack_heads(q, interleaved, scale=HEAD_DIM ** -0.5):
    q = (q.astype(F32) * scale).astype(BF16)
    lane = lax.broadcasted_iota(jnp.int32, q.shape, 1)
    first = ((lane // (HEAD_DIM // 2)) % 2 == 0) if interleaved else (lane < HEAD_DIM)
    zero = jnp.zeros_like(q)
    return jnp.concatenate([jnp.where(first, q, zero), jnp.where(first, zero, q)], axis=0)


def _own_heads(o, tq):
    return o[:HEAD_DIM, :tq], o[HEAD_DIM:, tq:]


def _store_heads(o_ref, rows, o_a, o_b):
    o_ref[0, rows, :] = jnp.concatenate([o_a, o_b], axis=0).T.astype(o_ref.dtype)


def _key_query_iota(blk):
    krow = lax.broadcasted_iota(jnp.int32, (blk, 2 * blk), 0)
    qcol = lax.broadcasted_iota(jnp.int32, (blk, 2 * blk), 1)
    return krow, jnp.where(qcol >= blk, qcol - blk, qcol)


def _colsum(a):
    return jnp.sum(a, axis=0, keepdims=True)


def _sb_attn_kernel(q_ref, k_ref, vt_ref, lmat_ref, o_ref, *, blk, ncb, nf):
    i = pl.program_id(2)
    cols = [slice(cb * LANES, (cb + 1) * LANES) for cb in range(ncb)]
    nq2 = [_stack_heads(q_ref[0, :, c], interleaved=False, scale=-SOFTMAX_SCALE_LOG2) for c in cols]
    lmat = lmat_ref[...]
    krow, qidx = _key_query_iota(blk)
    past = krow < qidx

    def k_block(j, c):
        return k_ref[0, pl.ds(pl.multiple_of(j * blk, blk), blk), c]

    def log_terms(k_rows, nq):
        n = lax.dot_general(k_rows, nq, NT_DIMS, preferred_element_type=F32)
        sp = jnp.log(1.0 + jnp.exp2(-jnp.abs(n))) * LOG2E
        log_keep = jnp.minimum(n, 0.0) - sp
        return log_keep, log_keep - n

    def later_keys(log_keep):
        return jnp.dot(lmat, log_keep.astype(BF16), preferred_element_type=F32)

    def live(carries):
        return (jnp.max(functools.reduce(jnp.maximum, carries)) > -SB_STOP * LOG2E).astype(jnp.int32)

    dist = list(range(nf - 1, -1, -1))
    ids = [jnp.maximum(i - d, 0) for d in dist]
    exists = [jnp.where(i >= d, 0.0, -jnp.inf) for d in dist]
    terms = [log_terms(jnp.concatenate([k_block(j, c) for j in ids], axis=0), nq)
             for c, nq in zip(cols, nq2)]
    keeps = [[jnp.where(past, lk[g * blk:(g + 1) * blk], 0.0) if d == 0 else lk[g * blk:(g + 1) * blk]
              for g, d in enumerate(dist)] for lk, _ in terms]
    laters = [[later_keys(lk_g) for lk_g in blocks] for blocks in keeps]
    probs, carries = [], []
    for (_, lb), blocks, lts in zip(terms, keeps, laters):
        carry = None
        parts = [None] * nf
        for g in range(nf - 1, -1, -1):
            expo = lb[g * blk:(g + 1) * blk] + lts[g]
            if dist[g] == 0:
                parts[g] = jnp.where(past, jnp.exp2(expo), 0.0)
                carry = _colsum(blocks[g])
            else:
                parts[g] = jnp.exp2(expo + (carry + exists[g]))
                carry = carry + _colsum(blocks[g])
        probs.append(jnp.concatenate(parts, axis=0).astype(BF16))
        carries.append(carry)
    accs = [jnp.dot(jnp.concatenate([vt_ref[0, j, c, :] for j in ids], axis=1), p,
                    preferred_element_type=F32) for c, p in zip(cols, probs)]

    def cond(st):
        return jnp.logical_and(st[0] >= 0, st[1] > 0)

    def body(st):
        j, carries, accs = st[0], st[2:2 + ncb], st[2 + ncb:]
        terms = [log_terms(k_block(j, c), nq) for c, nq in zip(cols, nq2)]
        laters = [later_keys(lk) for lk, _ in terms]
        probs = [jnp.exp2(lb + lt + cr).astype(BF16) for (_, lb), lt, cr in zip(terms, laters, carries)]
        accs = [acc + jnp.dot(vt_ref[0, j, c, :], p, preferred_element_type=F32)
                for acc, c, p in zip(accs, cols, probs)]
        carries = [cr + _colsum(lk) for cr, (lk, _) in zip(carries, terms)]
        return (j - 1, live(carries), *carries, *accs)

    st = lax.while_loop(cond, body, (i - nf, live(carries), *carries, *accs))
    for cb, c in enumerate(cols):
        o_a, o_b = _own_heads(st[2 + ncb + cb], blk)
        o_ref[0, :, c] = jnp.concatenate([o_a, o_b], axis=0).T.astype(o_ref.dtype)


def _sb_attention(qk, vt, blk):
    b, s, _ = qk.shape
    nb = s // blk
    ncb = SB_COL_BLOCKS
    row = np.arange(blk)
    lmat = jnp.asarray(row[None, :] > row[:, None], BF16)
    once = pl.Buffered(1)
    return pl.pallas_call(
        functools.partial(_sb_attn_kernel, blk=blk, ncb=ncb, nf=min(SB_FIRST_BLOCKS, nb)),
        out_shape=jax.ShapeDtypeStruct((b, s, D_MODEL), BF16),
        grid=(b, N_QBLK // ncb, nb),
        in_specs=[
            pl.BlockSpec((1, blk, ncb * LANES), lambda bi, c, i: (bi, i, c)),
            pl.BlockSpec((1, s, ncb * LANES), lambda bi, c, i: (bi, 0, N_QBLK // ncb + c),
                         pipeline_mode=once),
            pl.BlockSpec((1, nb, ncb * LANES, blk), lambda bi, c, i: (bi, 0, c, 0), pipeline_mode=once),
            _resident((blk, blk)),
        ],
        out_specs=pl.BlockSpec((1, blk, ncb * LANES), lambda bi, c, i: (bi, i, c)),
        compiler_params=_compiler_params(3),
        name="sb_attention",
    )(qk, qk, vt, lmat)


def _moba_attn_kernel(q_ref, k_ref, vt_ref, avg_ref, o_ref, kmean_ref, *, blk, kb, ncb):
    i = pl.program_id(2)
    nb = vt_ref.shape[1]
    nbp = avg_ref.shape[0]
    cols = [slice(cb * LANES, (cb + 1) * LANES) for cb in range(ncb)]

    @pl.when(i == 0)
    def _():
        kmean_ref[...] = jnp.dot(avg_ref[...], k_ref[0], preferred_element_type=F32)

    q2 = [_stack_heads(q_ref[0, :, c], interleaved=True, scale=SOFTMAX_SCALE_LOG2) for c in cols]
    krow, qidx = _key_query_iota(blk)
    causal = krow <= qidx
    nidx = lax.broadcasted_iota(jnp.int32, (nbp, 2 * blk), 0)
    neg_inf = jnp.float32(-jnp.inf)

    picks = []
    for c, q in zip(cols, q2):
        km = jnp.concatenate(_split_bf16(kmean_ref[:, c]), axis=1)
        gate = lax.dot_general(km, jnp.concatenate([q, q], axis=1), NT_DIMS,
                               preferred_element_type=F32)
        gate = jnp.where(nidx < i, gate, neg_inf)
        chosen = []
        for _ in range(MOBA_TOPK):
            top = jnp.max(gate, axis=0, keepdims=True)
            idx = jnp.min(jnp.where(gate == top, nidx, nbp), axis=0, keepdims=True)
            chosen.append(jnp.where(top > neg_inf, idx, -1))
            gate = jnp.where(nidx == idx, neg_inf, gate)
        picks.append(chosen)

    def score_stage(ids, own_first):
        load_ids = [jnp.minimum(bid, nb - 1) for bid in ids]
        scores = []
        for cb, c in enumerate(cols):
            k_rows = jnp.concatenate(
                [k_ref[0, pl.ds(pl.multiple_of(bid * blk, blk), blk), c] for bid in load_ids], axis=0)
            s = lax.dot_general(k_rows, q2[cb], NT_DIMS, preferred_element_type=F32)
            parts = []
            for g, bid in enumerate(ids):
                if own_first and g == 0:
                    mask = causal
                else:
                    mask = functools.reduce(jnp.logical_or, [p == bid for p in picks[cb]])
                parts.append(jnp.where(mask, s[g * blk:(g + 1) * blk], neg_inf))
            scores.append(jnp.concatenate(parts, axis=0))
        return load_ids, scores

    def softmax_stage(scores, st):
        soft = []
        for cb, s in enumerate(scores):
            m, l = st[4 * cb], st[4 * cb + 1]
            m_new = jnp.maximum(m, jnp.max(s, axis=0, keepdims=True))
            p = jnp.exp2(s - m_new)
            alpha = jnp.exp2(m - m_new)
            soft.append((m_new, alpha * l + _colsum(p), alpha, p.astype(BF16)))
        return soft

    def value_stage(load_ids, soft, st):
        out = []
        for cb, (c, (m_new, l_new, alpha, p)) in enumerate(zip(cols, soft)):
            v_cat = jnp.concatenate([vt_ref[0, bid, c, :] for bid in load_ids], axis=1)
            o_a, o_b = _own_heads(jnp.dot(v_cat, p, preferred_element_type=F32), blk)
            out += [m_new, l_new,
                    alpha[:, :blk] * st[4 * cb + 2] + o_a, alpha[:, blk:] * st[4 * cb + 3] + o_b]
        return tuple(out)

    def step(ids, own_first, st):
        load_ids, scores = score_stage(ids, own_first)
        return value_stage(load_ids, softmax_stage(scores, st), st)

    def past_ids(t):
        return [(kb - 1) + t * kb + g for g in range(kb)]

    st = (jnp.full((1, 2 * blk), neg_inf, F32), jnp.zeros((1, 2 * blk), F32),
          jnp.zeros((HEAD_DIM, blk), F32), jnp.zeros((HEAD_DIM, blk), F32)) * ncb
    st = step([i] + list(range(kb - 1)), True, st)
    n_steps = (jnp.maximum(i - (kb - 1), 0) + kb - 1) // kb

    def two_steps(t2, st):
        ids_a, scores_a = score_stage(past_ids(2 * t2), False)
        soft_a = softmax_stage(scores_a, st)
        ids_b, scores_b = score_stage(past_ids(2 * t2 + 1), False)
        st = value_stage(ids_a, soft_a, st)
        return value_stage(ids_b, softmax_stage(scores_b, st), st)

    st = lax.fori_loop(0, n_steps // 2, two_steps, st)
    st = lax.fori_loop(0, n_steps % 2, lambda _, st: step(past_ids(n_steps - 1), False, st), st)
    for cb, c in enumerate(cols):
        l, acc_a, acc_b = st[4 * cb + 1:4 * cb + 4]
        o_ref[0, :, c] = jnp.concatenate([acc_a / l[:, :blk], acc_b / l[:, blk:]],
                                         axis=0).T.astype(o_ref.dtype)


def _moba_attention(qk, vt):
    b, s, _ = qk.shape
    blk = MOBA_BLOCK
    nb = s // blk
    nbp = -(-nb // 8) * 8
    ncb = MOBA_COL_BLOCKS
    blk_of_key = np.arange(s) // blk
    avg = jnp.asarray((np.arange(nbp)[:, None] == blk_of_key[None, :]) * (1.0 / blk), BF16)
    return pl.pallas_call(
        functools.partial(_moba_attn_kernel, blk=blk, kb=min(MOBA_TILE_BLOCKS, nb), ncb=ncb),
        out_shape=jax.ShapeDtypeStruct((b, s, D_MODEL), BF16),
        grid=(b, N_QBLK // ncb, nb),
        in_specs=[
            pl.BlockSpec((1, blk, ncb * LANES), lambda bi, c, i: (bi, i, c)),
            pl.BlockSpec((1, s, ncb * LANES), lambda bi, c, i: (bi, 0, N_QBLK // ncb + c)),
            pl.BlockSpec((1, nb, ncb * LANES, blk), lambda bi, c, i: (bi, 0, c, 0)),
            _resident((nbp, s)),
        ],
        out_specs=pl.BlockSpec((1, blk, ncb * LANES), lambda bi, c, i: (bi, i, c)),
        scratch_shapes=[pltpu.VMEM((nbp, ncb * LANES), F32)],
        compiler_params=_compiler_params(3),
        name="moba_attention",
    )(qk, qk, vt, avg)


def _swa_attn_kernel(sink_ref, q_ref, k_ref, vt_ref, o_ref, *, blk, nsub):
    c = pl.program_id(1)
    i = pl.program_id(2)
    krow, qidx = _key_query_iota(blk)
    in_prev = krow > qidx
    in_own = krow <= qidx
    head_b = lax.broadcasted_iota(jnp.int32, (1, 2 * blk), 1) >= blk
    sink = jnp.where(head_b, sink_ref[c + N_QBLK], sink_ref[c]) * LOG2E
    neg_inf = jnp.float32(-jnp.inf)

    blocks = [(i * nsub + u, jnp.maximum(i * nsub + u - 1, 0)) for u in range(nsub)]
    scores = []
    for u, (g, gp) in enumerate(blocks):
        q2 = _stack_heads(q_ref[0, u * blk:(u + 1) * blk, :], interleaved=True, scale=SOFTMAX_SCALE_LOG2)
        k_rows = jnp.concatenate(
            [k_ref[0, pl.ds(pl.multiple_of(gp * blk, blk), blk), :],
             k_ref[0, pl.ds(pl.multiple_of(g * blk, blk), blk), :]], axis=0)
        s = lax.dot_general(k_rows, q2, NT_DIMS, preferred_element_type=F32)
        scores.append(jnp.concatenate(
            [jnp.where(jnp.logical_and(in_prev, g > 0), s[:blk], neg_inf),
             jnp.where(in_own, s[blk:], neg_inf)], axis=0))
    probs = []
    for s in scores:
        m = jnp.maximum(jnp.max(s, axis=0, keepdims=True), sink)
        p = jnp.exp2(s - m)
        probs.append((p.astype(BF16), _colsum(p) + jnp.exp2(sink - m)))
    for u, ((g, gp), (p, denom)) in enumerate(zip(blocks, probs)):
        v_cat = jnp.concatenate([vt_ref[0, gp], vt_ref[0, g]], axis=1)
        o_a, o_b = _own_heads(jnp.dot(v_cat, p, preferred_element_type=F32), blk)
        _store_heads(o_ref, slice(u * blk, (u + 1) * blk), o_a / denom[:, :blk], o_b / denom[:, blk:])


def _swa_attention(qk, vt, sinks):
    b, s, _ = qk.shape
    blk = SWA_WINDOW
    nb = s // blk
    tq = min(SWA_Q_TILE, s)
    return pl.pallas_call(
        functools.partial(_swa_attn_kernel, blk=blk, nsub=tq // blk),
        out_shape=jax.ShapeDtypeStruct((b, s, D_MODEL), BF16),
        grid=(b, N_QBLK, s // tq),
        in_specs=[
            pl.BlockSpec(memory_space=pltpu.SMEM),
            pl.BlockSpec((1, tq, LANES), lambda bi, c, i: (bi, i, c)),
            pl.BlockSpec((1, s, LANES), lambda bi, c, i: (bi, 0, N_QBLK)),
            pl.BlockSpec((1, nb, LANES, blk), lambda bi, c, i: (bi, 0, 0, 0)),
        ],
        out_specs=pl.BlockSpec((1, tq, LANES), lambda bi, c, i: (bi, i, c)),
        compiler_params=_compiler_params(3),
        name="swa_attention",
    )(sinks, qk, qk, vt)


def _out_ffn_kernel(attn_ref, x_ref, mod_ref, gain_ref, wo_ref, wg_ref, wu_ref, wd_ref, o_ref):
    y = jnp.dot(attn_ref[0], wo_ref[...], preferred_element_type=F32)
    x1 = x_ref[0] + mod_ref[0, 0, 2:3, :] * y
    h = _norm_modulate(x1, gain_ref[...], mod_ref[0, 0, 3:4, :], mod_ref[0, 0, 4:5, :])
    hb = h.astype(BF16)
    dff = wg_ref.shape[1]
    n_tiles = dff // MXU_WIDTH
    edges = [MXU_WIDTH * ((n_tiles * ck + FFN_CHUNKS - 1) // FFN_CHUNKS) for ck in range(FFN_CHUNKS)] + [dff]
    ff = None
    for ck in range(FFN_CHUNKS):
        cs = slice(edges[ck], edges[ck + 1])
        gate = jnp.dot(hb, wg_ref[:, cs], preferred_element_type=F32)
        up = jnp.dot(hb, wu_ref[:, cs], preferred_element_type=F32)
        act = (gate * (1.0 / (1.0 + jnp.exp(-gate)))) * up
        part = jnp.dot(act.astype(BF16), wd_ref[cs, :], preferred_element_type=F32)
        ff = part if ff is None else ff + part
    o_ref[0] = x1 + mod_ref[0, 0, 5:6, :] * ff


def _out_ffn(attn, x, mod, layer, gain, wo, wg, wu, wd):
    b, s, d = x.shape
    dff = wg.shape[1]
    tm = min(FFN_ROW_TILE, s)
    return pl.pallas_call(
        _out_ffn_kernel,
        out_shape=jax.ShapeDtypeStruct((b, s, d), F32),
        grid=(b, s // tm),
        in_specs=[
            pl.BlockSpec((1, tm, d), lambda bi, i: (bi, i, 0)),
            pl.BlockSpec((1, tm, d), lambda bi, i: (bi, i, 0)),
            pl.BlockSpec((1, 1, 6, d), lambda bi, i: (layer, bi, 0, 0)),
            _resident((1, d)),
            _resident((d, d)),
            _resident((d, dff)),
            _resident((d, dff)),
            _resident((dff, d)),
        ],
        out_specs=pl.BlockSpec((1, tm, d), lambda bi, i: (bi, i, 0)),
        compiler_params=_compiler_params(2),
        name=f"out_ffn_l{layer}",
    )(attn, x, mod, gain.reshape(1, d), wo, wg, wu, wd)


def _interleave_perm(head_a, head_b):
    half = HEAD_DIM // 2
    r = np.arange(half)
    return np.concatenate([head_a * HEAD_DIM + r, head_b * HEAD_DIM + r,
                           head_a * HEAD_DIM + half + r, head_b * HEAD_DIM + half + r])


def _rope_tables(seq_len):
    inv_freq = 1.0 / (ROPE_THETA ** (jnp.arange(0, HEAD_DIM, 2, dtype=F32) / HEAD_DIM))
    ang = jnp.arange(seq_len, dtype=F32)[:, None] * inv_freq[None, :]
    cos, sin = jnp.cos(ang), jnp.sin(ang)
    return (jnp.concatenate([cos] * 4, axis=1),
            jnp.concatenate([-sin, -sin, sin, sin], axis=1))


def _head_sum_matrix():
    half = HEAD_DIM // 2
    head = (np.arange(LANES) // half) % 2
    g = (head[:, None] == head[None, :]).astype(np.float32)
    return jnp.asarray(np.concatenate([g, g], axis=0), BF16)


def _block_gain(qk_gain):
    half = HEAD_DIM // 2
    lane = np.arange(LANES)
    dim = (lane % half) + half * (lane // (2 * half))
    return qk_gain[:, dim]


def kernel(x, c, ada_w, ada_b, norm_gain, ffn_w_gate, ffn_w_up, ffn_w_down, sb_w_in, sb_w_out, moba_w_in, moba_qk_gain, moba_w_out, swa_w_in, swa_qk_gain, swa_sinks, swa_w_out):
    b, s, d = x.shape
    depth = ada_w.shape[0]
    hd_all = N_HEADS * HEAD_DIM
    mod = _ada_modulation(c, ada_w, ada_b).reshape(depth, b, 6, d)
    cos_t, sin_t = _rope_tables(s)
    gsum = _head_sum_matrix()

    moba_cols = np.concatenate(
        [_interleave_perm(2 * cb, 2 * cb + 1) for cb in range(N_QBLK)]
        + [hd_all + _interleave_perm(2 * cb, 2 * cb + 1) for cb in range(N_QBLK)])
    swa_cols = np.concatenate(
        [_interleave_perm(cb, N_QBLK + cb) for cb in range(N_QBLK)]
        + [hd_all + _interleave_perm(0, 1)])
    swa_out_rows = np.concatenate(
        [np.concatenate([np.arange(HEAD_DIM) + cb * HEAD_DIM,
                         np.arange(HEAD_DIM) + (N_QBLK + cb) * HEAD_DIM]) for cb in range(N_QBLK)])

    for i in range(depth):
        kind, j = i % N_MIXERS, i // N_MIXERS
        if kind == 0:
            w_in = sb_w_in[j]
            wqk = w_in[:, :2 * hd_all].astype(BF16)
            wvt = w_in[:, 2 * hd_all:].T.astype(BF16)
            qk, vt = _norm_proj(x, mod, i, norm_gain[i, 0], wqk, wvt, SB_BLOCK)
            attn = _sb_attention(qk, vt, SB_BLOCK)
            wo = sb_w_out[j].astype(BF16)
        elif kind == 1:
            w_in = moba_w_in[j]
            wqk = w_in[:, moba_cols].astype(BF16)
            wvt = w_in[:, 2 * hd_all:].T.astype(BF16)
            rope = (cos_t, sin_t, gsum, _block_gain(moba_qk_gain[j]))
            qk, vt = _norm_proj(x, mod, i, norm_gain[i, 0], wqk, wvt, MOBA_BLOCK, rope)
            attn = _moba_attention(qk, vt)
            wo = moba_w_out[j].astype(BF16)
        else:
            w_in = swa_w_in[j]
            wqk = w_in[:, swa_cols].astype(BF16)
            wvt = w_in[:, hd_all + SWA_KV_HEADS * HEAD_DIM:].T.astype(BF16)
            rope = (cos_t, sin_t, gsum, _block_gain(swa_qk_gain[j]))
            qk, vt = _norm_proj(x, mod, i, norm_gain[i, 0], wqk, wvt, SWA_WINDOW, rope)
            attn = _swa_attention(qk, vt, swa_sinks[j])
            wo = swa_w_out[j][swa_out_rows, :].astype(BF16)
        x = _out_ffn(attn, x, mod, i, norm_gain[i, 1], wo,
                     ffn_w_gate[i].astype(BF16), ffn_w_up[i].astype(BF16), ffn_w_down[i].astype(BF16))
    return x
```

```python
import functools

import numpy as np
import jax
import jax.numpy as jnp
from jax import lax
from jax.experimental import pallas as pl
from jax.experimental.pallas import tpu as pltpu

D_MODEL = 1024
HEAD_DIM = 64
N_HEADS = D_MODEL // HEAD_DIM
SWA_KV_HEADS = 2
N_MIXERS = 3
MOBA_BLOCK = 256
MOBA_TOPK = 3
SWA_WINDOW = 128
ROPE_THETA = 10000.0
EPS = 1e-6

LANES = 128
N_QBLK = D_MODEL // LANES
VMEM_LIMIT_BYTES = 56 * 1024 * 1024
SB_BLOCK = 128
SB_FIRST_BLOCKS = 3
SB_STOP = 110.0
SB_COL_BLOCKS = 8
MOBA_TILE_BLOCKS = 2
MOBA_COL_BLOCKS = 4
SWA_Q_TILE = 1024
ROW_TILE = 512
FFN_ROW_TILE = 512
FFN_CHUNKS = 2
MXU_WIDTH = 256

LOG2E = 1.4426950408889634
SOFTMAX_SCALE_LOG2 = HEAD_DIM ** -0.5 * LOG2E

BF16 = jnp.bfloat16
F32 = jnp.float32
NT_DIMS = (((1,), (1,)), ((), ()))


def _compiler_params(n_grid):
    return pltpu.CompilerParams(
        dimension_semantics=("arbitrary",) * n_grid,
        vmem_limit_bytes=VMEM_LIMIT_BYTES)


def _resident(shape):
    nd = len(shape)
    return pl.BlockSpec(shape, lambda *_: (0,) * nd, pipeline_mode=pl.Buffered(1))


def _split_bf16(a):
    hi = a.astype(BF16)
    lo = (a - hi.astype(F32)).astype(BF16)
    return hi, lo


def _ada_kernel(c_ref, w_ref, b_ref, o_ref):
    c = c_ref[...]
    c_act = c * (1.0 / (1.0 + jnp.exp(-c)))
    o_ref[0] = jnp.dot(c_act, w_ref[0], preferred_element_type=F32,
                       precision=lax.Precision.HIGHEST) + b_ref[0]


def _ada_modulation(c, ada_w, ada_b):
    depth, d, n = ada_w.shape
    b = c.shape[0]
    tn = n // 4
    return pl.pallas_call(
        _ada_kernel,
        out_shape=jax.ShapeDtypeStruct((depth, b, n), F32),
        grid=(depth, n // tn),
        in_specs=[
            pl.BlockSpec((b, d), lambda l, j: (0, 0)),
            pl.BlockSpec((1, d, tn), lambda l, j: (l, 0, j)),
            pl.BlockSpec((1, 1, tn), lambda l, j: (l, 0, j)),
        ],
        out_specs=pl.BlockSpec((1, b, tn), lambda l, j: (l, 0, j)),
        compiler_params=_compiler_params(2),
        name="ada_modulation",
    )(c, ada_w, ada_b.reshape(depth, 1, n))


def _norm_modulate(x, gain, shift, scale):
    var = jnp.mean(x * x, axis=-1, keepdims=True)
    h = (x * lax.rsqrt(var + EPS)) * gain
    return h * (1.0 + scale) + shift


def _norm_proj_kernel(*refs, n_rope, tk):
    if n_rope:
        (x_ref, mod_ref, gain_ref, wqk_ref, wvt_ref,
         cos_ref, sin_ref, gsum_ref, qkg_ref, oqk_ref, ovt_ref) = refs
    else:
        x_ref, mod_ref, gain_ref, wqk_ref, wvt_ref, oqk_ref, ovt_ref = refs
    x = x_ref[0]
    h = _norm_modulate(x, gain_ref[...], mod_ref[0, 0, 0:1, :], mod_ref[0, 0, 1:2, :])
    hb = h.astype(BF16)
    qk = jnp.dot(hb, wqk_ref[...], preferred_element_type=F32)
    vt = lax.dot_general(wvt_ref[...], hb, NT_DIMS, preferred_element_type=F32)
    tm = x.shape[0]
    for t in range(tm // tk):
        ovt_ref[0, t] = vt[:, t * tk:(t + 1) * tk].astype(BF16)
    if not n_rope:
        oqk_ref[0] = qk.astype(BF16)
        return
    cos = cos_ref[...]
    sin = sin_ref[...]
    gsum = gsum_ref[...]
    for cb in range(n_rope):
        a = qk[:, cb * LANES:(cb + 1) * LANES]
        hi, lo = _split_bf16(a * a)
        ss = jnp.dot(jnp.concatenate([hi, lo], axis=1), gsum, preferred_element_type=F32)
        r = lax.rsqrt(ss * (1.0 / HEAD_DIM) + EPS)
        g = qkg_ref[0:1, :] if cb < N_QBLK else qkg_ref[1:2, :]
        xn = (a * r) * g
        out = xn * cos + pltpu.roll(xn, LANES // 2, 1) * sin
        oqk_ref[0, :, cb * LANES:(cb + 1) * LANES] = out.astype(BF16)


def _norm_proj(x, mod, layer, gain, wqk, wvt, tk, rope=None):
    b, s, d = x.shape
    nqk = wqk.shape[1]
    nv = wvt.shape[0]
    tm = min(ROW_TILE, s)
    n_rope = 0 if rope is None else nqk // LANES
    in_specs = [
        pl.BlockSpec((1, tm, d), lambda bi, i: (bi, i, 0)),
        pl.BlockSpec((1, 1, 6, d), lambda bi, i: (layer, bi, 0, 0)),
        _resident((1, d)),
        _resident((d, nqk)),
        _resident((nv, d)),
    ]
    args = [x, mod, gain.reshape(1, d), wqk, wvt]
    if rope is not None:
        cos_t, sin_t, gsum, qkg = rope
        in_specs += [
            pl.BlockSpec((tm, LANES), lambda bi, i: (i, 0)),
            pl.BlockSpec((tm, LANES), lambda bi, i: (i, 0)),
            _resident((2 * LANES, LANES)),
            _resident((2, LANES)),
        ]
        args += [cos_t, sin_t, gsum, qkg]
    return pl.pallas_call(
        functools.partial(_norm_proj_kernel, n_rope=n_rope, tk=tk),
        out_shape=(jax.ShapeDtypeStruct((b, s, nqk), BF16),
                   jax.ShapeDtypeStruct((b, s // tk, nv, tk), BF16)),
        grid=(b, s // tm),
        in_specs=in_specs,
        out_specs=(pl.BlockSpec((1, tm, nqk), lambda bi, i: (bi, i, 0)),
                   pl.BlockSpec((1, tm // tk, nv, tk), lambda bi, i: (bi, i, 0, 0))),
        compiler_params=_compiler_params(2),
        name=f"norm_proj_l{layer}",
    )(*args)


def _stack_heads(q, interleaved, scale=HEAD_DIM ** -0.5):
    q = (q.astype(F32) * scale).astype(BF16)
    lane = lax.broadcasted_iota(jnp.int32, q.shape, 1)
    first = ((lane // (HEAD_DIM // 2)) % 2 == 0) if interleaved else (lane < HEAD_DIM)
    zero = jnp.zeros_like(q)
    return jnp.concatenate([jnp.where(first, q, zero), jnp.where(first, zero, q)], axis=0)


def _own_heads(o, tq):
    return o[:HEAD_DIM, :tq], o[HEAD_DIM:, tq:]


def _store_heads(o_ref, rows, o_a, o_b):
    o_ref[0, rows, :] = jnp.concatenate([o_a, o_b], axis=0).T.astype(o_ref.dtype)


def _key_query_iota(blk):
    krow = lax.broadcasted_iota(jnp.int32, (blk, 2 * blk), 0)
    qcol = lax.broadcasted_iota(jnp.int32, (blk, 2 * blk), 1)
    return krow, jnp.where(qcol >= blk, qcol - blk, qcol)


def _colsum(a):
    return jnp.sum(a, axis=0, keepdims=True)


def _sb_attn_kernel(q_ref, k_ref, vt_ref, lmat_ref, o_ref, *, blk, ncb, nf):
    i = pl.program_id(2)
    cols = [slice(cb * LANES, (cb + 1) * LANES) for cb in range(ncb)]
    nq2 = [_stack_heads(q_ref[0, :, c], interleaved=False, scale=-SOFTMAX_SCALE_LOG2) for c in cols]
    lmat = lmat_ref[...]
    krow, qidx = _key_query_iota(blk)
    past = krow < qidx

    def k_block(j, c):
        return k_ref[0, pl.ds(pl.multiple_of(j * blk, blk), blk), c]

    def log_terms(k_rows, nq):
        n = lax.dot_general(k_rows, nq, NT_DIMS, preferred_element_type=F32)
        sp = jnp.log(1.0 + jnp.exp2(-jnp.abs(n))) * LOG2E
        log_keep = jnp.minimum(n, 0.0) - sp
        return log_keep, log_keep - n

    def later_keys(log_keep):
        return jnp.dot(lmat, log_keep.astype(BF16), preferred_element_type=F32)

    def live(carries):
        return (jnp.max(functools.reduce(jnp.maximum, carries)) > -SB_STOP * LOG2E).astype(jnp.int32)

    dist = list(range(nf - 1, -1, -1))
    ids = [jnp.maximum(i - d, 0) for d in dist]
    exists = [jnp.where(i >= d, 0.0, -jnp.inf) for d in dist]
    terms = [log_terms(jnp.concatenate([k_block(j, c) for j in ids], axis=0), nq)
             for c, nq in zip(cols, nq2)]
    keeps = [[jnp.where(past, lk[g * blk:(g + 1) * blk], 0.0) if d == 0 else lk[g * blk:(g + 1) * blk]
              for g, d in enumerate(dist)] for lk, _ in terms]
    laters = [[later_keys(lk_g) for lk_g in blocks] for blocks in keeps]
    probs, carries = [], []
    for (_, lb), blocks, lts in zip(terms, keeps, laters):
        carry = None
        parts = [None] * nf
        for g in range(nf - 1, -1, -1):
            expo = lb[g * blk:(g + 1) * blk] + lts[g]
            if dist[g] == 0:
                parts[g] = jnp.where(past, jnp.exp2(expo), 0.0)
                carry = _colsum(blocks[g])
            else:
                parts[g] = jnp.exp2(expo + (carry + exists[g]))
                carry = carry + _colsum(blocks[g])
        probs.append(jnp.concatenate(parts, axis=0).astype(BF16))
        carries.append(carry)
    accs = [jnp.dot(jnp.concatenate([vt_ref[0, j, c, :] for j in ids], axis=1), p,
                    preferred_element_type=F32) for c, p in zip(cols, probs)]

    def cond(st):
        return jnp.logical_and(st[0] >= 0, st[1] > 0)

    def body(st):
        j, carries, accs = st[0], st[2:2 + ncb], st[2 + ncb:]
        terms = [log_terms(k_block(j, c), nq) for c, nq in zip(cols, nq2)]
        laters = [later_keys(lk) for lk, _ in terms]
        probs = [jnp.exp2(lb + lt + cr).astype(BF16) for (_, lb), lt, cr in zip(terms, laters, carries)]
        accs = [acc + jnp.dot(vt_ref[0, j, c, :], p, preferred_element_type=F32)
                for acc, c, p in zip(accs, cols, probs)]
        carries = [cr + _colsum(lk) for cr, (lk, _) in zip(carries, terms)]
        return (j - 1, live(carries), *carries, *accs)

    st = lax.while_loop(cond, body, (i - nf, live(carries), *carries, *accs))
    for cb, c in enumerate(cols):
        o_a, o_b = _own_heads(st[2 + ncb + cb], blk)
        o_ref[0, :, c] = jnp.concatenate([o_a, o_b], axis=0).T.astype(o_ref.dtype)


def _sb_attention(qk, vt, blk):
    b, s, _ = qk.shape
    nb = s // blk
    ncb = SB_COL_BLOCKS
    row = np.arange(blk)
    lmat = jnp.asarray(row[None, :] > row[:, None], BF16)
    once = pl.Buffered(1)
    return pl.pallas_call(
        functools.partial(_sb_attn_kernel, blk=blk, ncb=ncb, nf=min(SB_FIRST_BLOCKS, nb)),
        out_shape=jax.ShapeDtypeStruct((b, s, D_MODEL), BF16),
        grid=(b, N_QBLK // ncb, nb),
        in_specs=[
            pl.BlockSpec((1, blk, ncb * LANES), lambda bi, c, i: (bi, i, c)),
            pl.BlockSpec((1, s, ncb * LANES), lambda bi, c, i: (bi, 0, N_QBLK // ncb + c),
                         pipeline_mode=once),
            pl.BlockSpec((1, nb, ncb * LANES, blk), lambda bi, c, i: (bi, 0, c, 0), pipeline_mode=once),
            _resident((blk, blk)),
        ],
        out_specs=pl.BlockSpec((1, blk, ncb * LANES), lambda bi, c, i: (bi, i, c)),
        compiler_params=_compiler_params(3),
        name="sb_attention",
    )(qk, qk, vt, lmat)


def _moba_attn_kernel(q_ref, k_ref, vt_ref, avg_ref, o_ref, kmean_ref, *, blk, kb, ncb):
    i = pl.program_id(2)
    nb = vt_ref.shape[1]
    nbp = avg_ref.shape[0]
    cols = [slice(cb * LANES, (cb + 1) * LANES) for cb in range(ncb)]

    @pl.when(i == 0)
    def _():
        kmean_ref[...] = jnp.dot(avg_ref[...], k_ref[0], preferred_element_type=F32)

    q2 = [_stack_heads(q_ref[0, :, c], interleaved=True, scale=SOFTMAX_SCALE_LOG2) for c in cols]
    krow, qidx = _key_query_iota(blk)
    causal = krow <= qidx
    nidx = lax.broadcasted_iota(jnp.int32, (nbp, 2 * blk), 0)
    neg_inf = jnp.float32(-jnp.inf)
    neg_inf_b = jnp.asarray(-jnp.inf, BF16)

    picks = []
    for c, q in zip(cols, q2):
        km = jnp.concatenate(_split_bf16(kmean_ref[:, c]), axis=1)
        gate = lax.dot_general(km, jnp.concatenate([q, q], axis=1), NT_DIMS,
                               preferred_element_type=F32)
        gate = jnp.where(nidx < i, gate, neg_inf)
        chosen = []
        for _ in range(MOBA_TOPK):
            top = jnp.max(gate, axis=0, keepdims=True)
            idx = jnp.min(jnp.where(gate == top, nidx, nbp), axis=0, keepdims=True)
            chosen.append(jnp.where(top > neg_inf, idx, -1))
            gate = jnp.where(nidx == idx, neg_inf, gate)
        picks.append(chosen)

    def score_stage(ids, own_first):
        load_ids = [jnp.minimum(bid, nb - 1) for bid in ids]
        scores = []
        for cb, c in enumerate(cols):
            k_rows = jnp.concatenate(
                [k_ref[0, pl.ds(pl.multiple_of(bid * blk, blk), blk), c] for bid in load_ids], axis=0)
            s = lax.dot_general(k_rows, q2[cb], NT_DIMS, preferred_element_type=F32)
            parts = []
            for g, bid in enumerate(ids):
                if own_first and g == 0:
                    mask = causal
                else:
                    mask = functools.reduce(jnp.logical_or, [p == bid for p in picks[cb]])
                parts.append(jnp.where(mask, s[g * blk:(g + 1) * blk].astype(BF16), neg_inf_b))
            scores.append(jnp.concatenate(parts, axis=0))
        return load_ids, scores

    def softmax_stage(scores, st):
        soft = []
        for cb, s in enumerate(scores):
            m, l = st[4 * cb], st[4 * cb + 1]
            m_new = jnp.maximum(m, jnp.max(s, axis=0, keepdims=True).astype(F32))
            p = jnp.exp2(s - m_new.astype(BF16))
            alpha = jnp.exp2(m - m_new)
            soft.append((m_new, l, alpha, p))
        return soft

    def value_stage(load_ids, soft, st):
        out = []
        for cb, (c, (m_new, l_old, alpha, p)) in enumerate(zip(cols, soft)):
            v_cat = jnp.concatenate([vt_ref[0, bid, c, :] for bid in load_ids], axis=1)
            v_aug = jnp.concatenate([v_cat, jnp.ones((8, v_cat.shape[1]), BF16)], axis=0)
            o = jnp.dot(v_aug, p, preferred_element_type=F32)
            o_a, o_b = _own_heads(o[:2 * HEAD_DIM], blk)
            l_new = alpha * l_old + o[2 * HEAD_DIM:2 * HEAD_DIM + 1]
            out += [m_new, l_new,
                    alpha[:, :blk] * st[4 * cb + 2] + o_a, alpha[:, blk:] * st[4 * cb + 3] + o_b]
        return tuple(out)

    def step(ids, own_first, st):
        load_ids, scores = score_stage(ids, own_first)
        return value_stage(load_ids, softmax_stage(scores, st), st)

    def past_ids(t):
        return [(kb - 1) + t * kb + g for g in range(kb)]

    st = (jnp.full((1, 2 * blk), neg_inf, F32), jnp.zeros((1, 2 * blk), F32),
          jnp.zeros((HEAD_DIM, blk), F32), jnp.zeros((HEAD_DIM, blk), F32)) * ncb
    st = step([i] + list(range(kb - 1)), True, st)
    n_steps = (jnp.maximum(i - (kb - 1), 0) + kb - 1) // kb

    def two_steps(t2, st):
        ids_a, scores_a = score_stage(past_ids(2 * t2), False)
        soft_a = softmax_stage(scores_a, st)
        ids_b, scores_b = score_stage(past_ids(2 * t2 + 1), False)
        st = value_stage(ids_a, soft_a, st)
        return value_stage(ids_b, softmax_stage(scores_b, st), st)

    st = lax.fori_loop(0, n_steps // 2, two_steps, st)
    st = lax.fori_loop(0, n_steps % 2, lambda _, st: step(past_ids(n_steps - 1), False, st), st)
    for cb, c in enumerate(cols):
        l, acc_a, acc_b = st[4 * cb + 1:4 * cb + 4]
        o_ref[0, :, c] = jnp.concatenate([acc_a / l[:, :blk], acc_b / l[:, blk:]],
                                         axis=0).T.astype(o_ref.dtype)


def _moba_attention(qk, vt):
    b, s, _ = qk.shape
    blk = MOBA_BLOCK
    nb = s // blk
    nbp = -(-nb // 8) * 8
    ncb = MOBA_COL_BLOCKS
    blk_of_key = np.arange(s) // blk
    avg = jnp.asarray((np.arange(nbp)[:, None] == blk_of_key[None, :]) * (1.0 / blk), BF16)
    return pl.pallas_call(
        functools.partial(_moba_attn_kernel, blk=blk, kb=min(MOBA_TILE_BLOCKS, nb), ncb=ncb),
        out_shape=jax.ShapeDtypeStruct((b, s, D_MODEL), BF16),
        grid=(b, N_QBLK // ncb, nb),
        in_specs=[
            pl.BlockSpec((1, blk, ncb * LANES), lambda bi, c, i: (bi, i, c)),
            pl.BlockSpec((1, s, ncb * LANES), lambda bi, c, i: (bi, 0, N_QBLK // ncb + c)),
            pl.BlockSpec((1, nb, ncb * LANES, blk), lambda bi, c, i: (bi, 0, c, 0)),
            _resident((nbp, s)),
        ],
        out_specs=pl.BlockSpec((1, blk, ncb * LANES), lambda bi, c, i: (bi, i, c)),
        scratch_shapes=[pltpu.VMEM((nbp, ncb * LANES), F32)],
        compiler_params=_compiler_params(3),
        name="moba_attention",
    )(qk, qk, vt, avg)


def _swa_attn_kernel(sink_ref, q_ref, k_ref, vt_ref, o_ref, *, blk, nsub):
    c = pl.program_id(1)
    i = pl.program_id(2)
    krow, qidx = _key_query_iota(blk)
    in_prev = krow > qidx
    in_own = krow <= qidx
    head_b = lax.broadcasted_iota(jnp.int32, (1, 2 * blk), 1) >= blk
    sink = jnp.where(head_b, sink_ref[c + N_QBLK], sink_ref[c]) * LOG2E
    neg_inf = jnp.float32(-jnp.inf)

    blocks = [(i * nsub + u, jnp.maximum(i * nsub + u - 1, 0)) for u in range(nsub)]
    scores = []
    for u, (g, gp) in enumerate(blocks):
        q2 = _stack_heads(q_ref[0, u * blk:(u + 1) * blk, :], interleaved=True, scale=SOFTMAX_SCALE_LOG2)
        k_rows = jnp.concatenate(
            [k_ref[0, pl.ds(pl.multiple_of(gp * blk, blk), blk), :],
             k_ref[0, pl.ds(pl.multiple_of(g * blk, blk), blk), :]], axis=0)
        s = lax.dot_general(k_rows, q2, NT_DIMS, preferred_element_type=F32)
        scores.append(jnp.concatenate(
            [jnp.where(jnp.logical_and(in_prev, g > 0), s[:blk], neg_inf),
             jnp.where(in_own, s[blk:], neg_inf)], axis=0))
    probs = []
    for s in scores:
        m = jnp.maximum(jnp.max(s, axis=0, keepdims=True), sink)
        p = jnp.exp2(s - m)
        probs.append((p.astype(BF16), _colsum(p) + jnp.exp2(sink - m)))
    for u, ((g, gp), (p, denom)) in enumerate(zip(blocks, probs)):
        v_cat = jnp.concatenate([vt_ref[0, gp], vt_ref[0, g]], axis=1)
        o_a, o_b = _own_heads(jnp.dot(v_cat, p, preferred_element_type=F32), blk)
        _store_heads(o_ref, slice(u * blk, (u + 1) * blk), o_a / denom[:, :blk], o_b / denom[:, blk:])


def _swa_attention(qk, vt, sinks):
    b, s, _ = qk.shape
    blk = SWA_WINDOW
    nb = s // blk
    tq = min(SWA_Q_TILE, s)
    return pl.pallas_call(
        functools.partial(_swa_attn_kernel, blk=blk, nsub=tq // blk),
        out_shape=jax.ShapeDtypeStruct((b, s, D_MODEL), BF16),
        grid=(b, N_QBLK, s // tq),
        in_specs=[
            pl.BlockSpec(memory_space=pltpu.SMEM),
            pl.BlockSpec((1, tq, LANES), lambda bi, c, i: (bi, i, c)),
            pl.BlockSpec((1, s, LANES), lambda bi, c, i: (bi, 0, N_QBLK)),
            pl.BlockSpec((1, nb, LANES, blk), lambda bi, c, i: (bi, 0, 0, 0)),
        ],
        out_specs=pl.BlockSpec((1, tq, LANES), lambda bi, c, i: (bi, i, c)),
        compiler_params=_compiler_params(3),
        name="swa_attention",
    )(sinks, qk, qk, vt)


def _out_ffn_kernel(attn_ref, x_ref, mod_ref, gain_ref, wo_ref, wg_ref, wu_ref, wd_ref, o_ref):
    y = jnp.dot(attn_ref[0], wo_ref[...], preferred_element_type=F32)
    x1 = x_ref[0] + mod_ref[0, 0, 2:3, :] * y
    h = _norm_modulate(x1, gain_ref[...], mod_ref[0, 0, 3:4, :], mod_ref[0, 0, 4:5, :])
    hb = h.astype(BF16)
    dff = wg_ref.shape[1]
    n_tiles = dff // MXU_WIDTH
    edges = [MXU_WIDTH * ((n_tiles * ck + FFN_CHUNKS - 1) // FFN_CHUNKS) for ck in range(FFN_CHUNKS)] + [dff]
    ff = None
    for ck in range(FFN_CHUNKS):
        cs = slice(edges[ck], edges[ck + 1])
        gate = jnp.dot(hb, wg_ref[:, cs], preferred_element_type=F32)
        up = jnp.dot(hb, wu_ref[:, cs], preferred_element_type=F32)
        act = (gate * (1.0 / (1.0 + jnp.exp(-gate)))) * up
        part = jnp.dot(act.astype(BF16), wd_ref[cs, :], preferred_element_type=F32)
        ff = part if ff is None else ff + part
    o_ref[0] = x1 + mod_ref[0, 0, 5:6, :] * ff


def _out_ffn(attn, x, mod, layer, gain, wo, wg, wu, wd):
    b, s, d = x.shape
    dff = wg.shape[1]
    tm = min(FFN_ROW_TILE, s)
    return pl.pallas_call(
        _out_ffn_kernel,
        out_shape=jax.ShapeDtypeStruct((b, s, d), F32),
        grid=(b, s // tm),
        in_specs=[
            pl.BlockSpec((1, tm, d), lambda bi, i: (bi, i, 0)),
            pl.BlockSpec((1, tm, d), lambda bi, i: (bi, i, 0)),
            pl.BlockSpec((1, 1, 6, d), lambda bi, i: (layer, bi, 0, 0)),
            _resident((1, d)),
            _resident((d, d)),
            _resident((d, dff)),
            _resident((d, dff)),
            _resident((dff, d)),
        ],
        out_specs=pl.BlockSpec((1, tm, d), lambda bi, i: (bi, i, 0)),
        compiler_params=_compiler_params(2),
        name=f"out_ffn_l{layer}",
    )(attn, x, mod, gain.reshape(1, d), wo, wg, wu, wd)


def _interleave_perm(head_a, head_b):
    half = HEAD_DIM // 2
    r = np.arange(half)
    return np.concatenate([head_a * HEAD_DIM + r, head_b * HEAD_DIM + r,
                           head_a * HEAD_DIM + half + r, head_b * HEAD_DIM + half + r])


def _rope_tables(seq_len):
    inv_freq = 1.0 / (ROPE_THETA ** (jnp.arange(0, HEAD_DIM, 2, dtype=F32) / HEAD_DIM))
    ang = jnp.arange(seq_len, dtype=F32)[:, None] * inv_freq[None, :]
    cos, sin = jnp.cos(ang), jnp.sin(ang)
    return (jnp.concatenate([cos] * 4, axis=1),
            jnp.concatenate([-sin, -sin, sin, sin], axis=1))


def _head_sum_matrix():
    half = HEAD_DIM // 2
    head = (np.arange(LANES) // half) % 2
    g = (head[:, None] == head[None, :]).astype(np.float32)
    return jnp.asarray(np.concatenate([g, g], axis=0), BF16)


def _block_gain(qk_gain):
    half = HEAD_DIM // 2
    lane = np.arange(LANES)
    dim = (lane % half) + half * (lane // (2 * half))
    return qk_gain[:, dim]


def kernel(x, c, ada_w, ada_b, norm_gain, ffn_w_gate, ffn_w_up, ffn_w_down, sb_w_in, sb_w_out, moba_w_in, moba_qk_gain, moba_w_out, swa_w_in, swa_qk_gain, swa_sinks, swa_w_out):
    b, s, d = x.shape
    depth = ada_w.shape[0]
    hd_all = N_HEADS * HEAD_DIM
    mod = _ada_modulation(c, ada_w, ada_b).reshape(depth, b, 6, d)
    cos_t, sin_t = _rope_tables(s)
    gsum = _head_sum_matrix()

    moba_cols = np.concatenate(
        [_interleave_perm(2 * cb, 2 * cb + 1) for cb in range(N_QBLK)]
        + [hd_all + _interleave_perm(2 * cb, 2 * cb + 1) for cb in range(N_QBLK)])
    swa_cols = np.concatenate(
        [_interleave_perm(cb, N_QBLK + cb) for cb in range(N_QBLK)]
        + [hd_all + _interleave_perm(0, 1)])
    swa_out_rows = np.concatenate(
        [np.concatenate([np.arange(HEAD_DIM) + cb * HEAD_DIM,
                         np.arange(HEAD_DIM) + (N_QBLK + cb) * HEAD_DIM]) for cb in range(N_QBLK)])

    for i in range(depth):
        kind, j = i % N_MIXERS, i // N_MIXERS
        if kind == 0:
            w_in = sb_w_in[j]
            wqk = w_in[:, :2 * hd_all].astype(BF16)
            wvt = w_in[:, 2 * hd_all:].T.astype(BF16)
            qk, vt = _norm_proj(x, mod, i, norm_gain[i, 0], wqk, wvt, SB_BLOCK)
            attn = _sb_attention(qk, vt, SB_BLOCK)
            wo = sb_w_out[j].astype(BF16)
        elif kind == 1:
            w_in = moba_w_in[j]
            wqk = w_in[:, moba_cols].astype(BF16)
            wvt = w_in[:, 2 * hd_all:].T.astype(BF16)
            rope = (cos_t, sin_t, gsum, _block_gain(moba_qk_gain[j]))
            qk, vt = _norm_proj(x, mod, i, norm_gain[i, 0], wqk, wvt, MOBA_BLOCK, rope)
            attn = _moba_attention(qk, vt)
            wo = moba_w_out[j].astype(BF16)
        else:
            w_in = swa_w_in[j]
            wqk = w_in[:, swa_cols].astype(BF16)
            wvt = w_in[:, hd_all + SWA_KV_HEADS * HEAD_DIM:].T.astype(BF16)
            rope = (cos_t, sin_t, gsum, _block_gain(swa_qk_gain[j]))
            qk, vt = _norm_proj(x, mod, i, norm_gain[i, 0], wqk, wvt, SWA_WINDOW, rope)
            attn = _swa_attention(qk, vt, swa_sinks[j])
            wo = swa_w_out[j][swa_out_rows, :].astype(BF16)
        x = _out_ffn(attn, x, mod, i, norm_gain[i, 1], wo,
                     ffn_w_gate[i].astype(BF16), ffn_w_up[i].astype(BF16), ffn_w_down[i].astype(BF16))
    return x
```

```python
import functools

import numpy as np
import jax
import jax.numpy as jnp
from jax import lax
from jax.experimental import pallas as pl
from jax.experimental.pallas import tpu as pltpu

D_MODEL = 1024
HEAD_DIM = 64
N_HEADS = D_MODEL // HEAD_DIM
SWA_KV_HEADS = 2
N_MIXERS = 3
MOBA_BLOCK = 256
MOBA_TOPK = 3
SWA_WINDOW = 128
ROPE_THETA = 10000.0
EPS = 1e-6

LANES = 128
N_QBLK = D_MODEL // LANES
VMEM_LIMIT_BYTES = 56 * 1024 * 1024
SB_BLOCK = 128
SB_FIRST_BLOCKS = 3
SB_STOP = 110.0
SB_COL_BLOCKS = 8
SB_WAVE = 4
MOBA_TILE_BLOCKS = 2
MOBA_COL_BLOCKS = 4
SWA_Q_TILE = 1024
ROW_TILE = 512
FFN_ROW_TILE = 512
FFN_CHUNKS = 2
MXU_WIDTH = 256

LOG2E = 1.4426950408889634
SOFTMAX_SCALE_LOG2 = HEAD_DIM ** -0.5 * LOG2E

BF16 = jnp.bfloat16
F32 = jnp.float32
NT_DIMS = (((1,), (1,)), ((), ()))


def _compiler_params(n_grid):
    return pltpu.CompilerParams(
        dimension_semantics=("arbitrary",) * n_grid,
        vmem_limit_bytes=VMEM_LIMIT_BYTES)


def _resident(shape):
    nd = len(shape)
    return pl.BlockSpec(shape, lambda *_: (0,) * nd, pipeline_mode=pl.Buffered(1))


def _split_bf16(a):
    hi = a.astype(BF16)
    lo = (a - hi.astype(F32)).astype(BF16)
    return hi, lo


def _ada_kernel(c_ref, w_ref, b_ref, o_ref):
    c = c_ref[...]
    c_act = c * (1.0 / (1.0 + jnp.exp(-c)))
    o_ref[0] = jnp.dot(c_act, w_ref[0], preferred_element_type=F32,
                       precision=lax.Precision.HIGHEST) + b_ref[0]


def _ada_modulation(c, ada_w, ada_b):
    depth, d, n = ada_w.shape
    b = c.shape[0]
    tn = n // 4
    return pl.pallas_call(
        _ada_kernel,
        out_shape=jax.ShapeDtypeStruct((depth, b, n), F32),
        grid=(depth, n // tn),
        in_specs=[
            pl.BlockSpec((b, d), lambda l, j: (0, 0)),
            pl.BlockSpec((1, d, tn), lambda l, j: (l, 0, j)),
            pl.BlockSpec((1, 1, tn), lambda l, j: (l, 0, j)),
        ],
        out_specs=pl.BlockSpec((1, b, tn), lambda l, j: (l, 0, j)),
        compiler_params=_compiler_params(2),
        name="ada_modulation",
    )(c, ada_w, ada_b.reshape(depth, 1, n))


def _norm_modulate(x, gain, shift, scale):
    var = jnp.mean(x * x, axis=-1, keepdims=True)
    h = (x * lax.rsqrt(var + EPS)) * gain
    return h * (1.0 + scale) + shift


def _norm_proj_kernel(*refs, n_rope, tk):
    if n_rope:
        (x_ref, mod_ref, gain_ref, wqk_ref, wvt_ref,
         cos_ref, sin_ref, gsum_ref, qkg_ref, oqk_ref, ovt_ref) = refs
    else:
        x_ref, mod_ref, gain_ref, wqk_ref, wvt_ref, oqk_ref, ovt_ref = refs
    x = x_ref[0]
    h = _norm_modulate(x, gain_ref[...], mod_ref[0, 0, 0:1, :], mod_ref[0, 0, 1:2, :])
    hb = h.astype(BF16)
    qk = jnp.dot(hb, wqk_ref[...], preferred_element_type=F32)
    vt = lax.dot_general(wvt_ref[...], hb, NT_DIMS, preferred_element_type=F32)
    tm = x.shape[0]
    for t in range(tm // tk):
        ovt_ref[0, t] = vt[:, t * tk:(t + 1) * tk].astype(BF16)
    if not n_rope:
        oqk_ref[0] = qk.astype(BF16)
        return
    cos = cos_ref[...]
    sin = sin_ref[...]
    gsum = gsum_ref[...]
    for cb in range(n_rope):
        a = qk[:, cb * LANES:(cb + 1) * LANES]
        hi, lo = _split_bf16(a * a)
        ss = jnp.dot(jnp.concatenate([hi, lo], axis=1), gsum, preferred_element_type=F32)
        r = lax.rsqrt(ss * (1.0 / HEAD_DIM) + EPS)
        g = qkg_ref[0:1, :] if cb < N_QBLK else qkg_ref[1:2, :]
        xn = (a * r) * g
        out = xn * cos + pltpu.roll(xn, LANES // 2, 1) * sin
        oqk_ref[0, :, cb * LANES:(cb + 1) * LANES] = out.astype(BF16)


def _norm_proj(x, mod, layer, gain, wqk, wvt, tk, rope=None):
    b, s, d = x.shape
    nqk = wqk.shape[1]
    nv = wvt.shape[0]
    tm = min(ROW_TILE, s)
    n_rope = 0 if rope is None else nqk // LANES
    in_specs = [
        pl.BlockSpec((1, tm, d), lambda bi, i: (bi, i, 0)),
        pl.BlockSpec((1, 1, 6, d), lambda bi, i: (layer, bi, 0, 0)),
        _resident((1, d)),
        _resident((d, nqk)),
        _resident((nv, d)),
    ]
    args = [x, mod, gain.reshape(1, d), wqk, wvt]
    if rope is not None:
        cos_t, sin_t, gsum, qkg = rope
        in_specs += [
            pl.BlockSpec((tm, LANES), lambda bi, i: (i, 0)),
            pl.BlockSpec((tm, LANES), lambda bi, i: (i, 0)),
            _resident((2 * LANES, LANES)),
            _resident((2, LANES)),
        ]
        args += [cos_t, sin_t, gsum, qkg]
    return pl.pallas_call(
        functools.partial(_norm_proj_kernel, n_rope=n_rope, tk=tk),
        out_shape=(jax.ShapeDtypeStruct((b, s, nqk), BF16),
                   jax.ShapeDtypeStruct((b, s // tk, nv, tk), BF16)),
        grid=(b, s // tm),
        in_specs=in_specs,
        out_specs=(pl.BlockSpec((1, tm, nqk), lambda bi, i: (bi, i, 0)),
                   pl.BlockSpec((1, tm // tk, nv, tk), lambda bi, i: (bi, i, 0, 0))),
        compiler_params=_compiler_params(2),
        name=f"norm_proj_l{layer}",
    )(*args)


def _stack_heads(q, interleaved, scale=HEAD_DIM ** -0.5):
    q = (q.astype(F32) * scale).astype(BF16)
    lane = lax.broadcasted_iota(jnp.int32, q.shape, 1)
    first = ((lane // (HEAD_DIM // 2)) % 2 == 0) if interleaved else (lane < HEAD_DIM)
    zero = jnp.zeros_like(q)
    return jnp.concatenate([jnp.where(first, q, zero), jnp.where(first, zero, q)], axis=0)


def _own_heads(o, tq):
    return o[:HEAD_DIM, :tq], o[HEAD_DIM:, tq:]


def _store_heads(o_ref, rows, o_a, o_b):
    o_ref[0, rows, :] = jnp.concatenate([o_a, o_b], axis=0).T.astype(o_ref.dtype)


def _key_query_iota(blk):
    krow = lax.broadcasted_iota(jnp.int32, (blk, 2 * blk), 0)
    qcol = lax.broadcasted_iota(jnp.int32, (blk, 2 * blk), 1)
    return krow, jnp.where(qcol >= blk, qcol - blk, qcol)


def _colsum(a):
    return jnp.sum(a, axis=0, keepdims=True)


def _sb_attn_kernel(q_ref, k_ref, vt_ref, lmat_ref, o_ref, *, blk, ncb, nf):
    i = pl.program_id(2)
    cols = [slice(cb * LANES, (cb + 1) * LANES) for cb in range(ncb)]
    nq2 = [_stack_heads(q_ref[0, :, c], interleaved=False, scale=-SOFTMAX_SCALE_LOG2) for c in cols]
    lmat = lmat_ref[...]
    krow, qidx = _key_query_iota(blk)
    past = krow < qidx

    def k_block(j, c):
        return k_ref[0, pl.ds(pl.multiple_of(j * blk, blk), blk), c]

    def log_terms(k_rows, nq):
        n = lax.dot_general(k_rows, nq, NT_DIMS, preferred_element_type=F32)
        sp = jnp.log(1.0 + jnp.exp2(-jnp.abs(n))) * LOG2E
        log_keep = jnp.minimum(n, 0.0) - sp
        return log_keep, log_keep - n

    def later_keys(log_keep):
        return jnp.dot(lmat, log_keep.astype(BF16), preferred_element_type=F32)

    def live(carries):
        return (jnp.max(functools.reduce(jnp.maximum, carries)) > -SB_STOP * LOG2E).astype(jnp.int32)

    dist = list(range(nf - 1, -1, -1))
    ids = [jnp.maximum(i - d, 0) for d in dist]
    exists = [jnp.where(i >= d, 0.0, -jnp.inf) for d in dist]
    def first_pass(wcols, wnq):
        terms = [log_terms(jnp.concatenate([k_block(j, c) for j in ids], axis=0), nq)
                 for c, nq in zip(wcols, wnq)]
        keeps = [[jnp.where(past, lk[g * blk:(g + 1) * blk], 0.0) if d == 0 else lk[g * blk:(g + 1) * blk]
                  for g, d in enumerate(dist)] for lk, _ in terms]
        laters = [[later_keys(lk_g) for lk_g in blocks] for blocks in keeps]
        probs, carries = [], []
        for (_, lb), blocks, lts in zip(terms, keeps, laters):
            carry = None
            parts = [None] * nf
            for g in range(nf - 1, -1, -1):
                expo = lb[g * blk:(g + 1) * blk] + lts[g]
                if dist[g] == 0:
                    parts[g] = jnp.where(past, jnp.exp2(expo), 0.0)
                    carry = _colsum(blocks[g])
                else:
                    parts[g] = jnp.exp2(expo + (carry + exists[g]))
                    carry = carry + _colsum(blocks[g])
            probs.append(jnp.concatenate(parts, axis=0).astype(BF16))
            carries.append(carry)
        accs = [jnp.dot(jnp.concatenate([vt_ref[0, j, c, :] for j in ids], axis=1), p,
                        preferred_element_type=F32) for c, p in zip(wcols, probs)]
        return carries, accs

    carries, accs = [], []
    for w0 in range(0, ncb, SB_WAVE):
        wc, wa = first_pass(cols[w0:w0 + SB_WAVE], nq2[w0:w0 + SB_WAVE])
        carries += wc
        accs += wa

    def cond(st):
        return jnp.logical_and(st[0] >= 0, st[1] > 0)

    def body(st):
        j, carries, accs = st[0], st[2:2 + ncb], st[2 + ncb:]
        terms = [log_terms(k_block(j, c), nq) for c, nq in zip(cols, nq2)]
        laters = [later_keys(lk) for lk, _ in terms]
        probs = [jnp.exp2(lb + lt + cr).astype(BF16) for (_, lb), lt, cr in zip(terms, laters, carries)]
        accs = [acc + jnp.dot(vt_ref[0, j, c, :], p, preferred_element_type=F32)
                for acc, c, p in zip(accs, cols, probs)]
        carries = [cr + _colsum(lk) for cr, (lk, _) in zip(carries, terms)]
        return (j - 1, live(carries), *carries, *accs)

    st = lax.while_loop(cond, body, (i - nf, live(carries), *carries, *accs))
    for cb, c in enumerate(cols):
        o_a, o_b = _own_heads(st[2 + ncb + cb], blk)
        o_ref[0, :, c] = jnp.concatenate([o_a, o_b], axis=0).T.astype(o_ref.dtype)


def _sb_attention(qk, vt, blk):
    b, s, _ = qk.shape
    nb = s // blk
    ncb = SB_COL_BLOCKS
    row = np.arange(blk)
    lmat = jnp.asarray(row[None, :] > row[:, None], BF16)
    once = pl.Buffered(1)
    return pl.pallas_call(
        functools.partial(_sb_attn_kernel, blk=blk, ncb=ncb, nf=min(SB_FIRST_BLOCKS, nb)),
        out_shape=jax.ShapeDtypeStruct((b, s, D_MODEL), BF16),
        grid=(b, N_QBLK // ncb, nb),
        in_specs=[
            pl.BlockSpec((1, blk, ncb * LANES), lambda bi, c, i: (bi, i, c)),
            pl.BlockSpec((1, s, ncb * LANES), lambda bi, c, i: (bi, 0, N_QBLK // ncb + c),
                         pipeline_mode=once),
            pl.BlockSpec((1, nb, ncb * LANES, blk), lambda bi, c, i: (bi, 0, c, 0), pipeline_mode=once),
            _resident((blk, blk)),
        ],
        out_specs=pl.BlockSpec((1, blk, ncb * LANES), lambda bi, c, i: (bi, i, c)),
        compiler_params=_compiler_params(3),
        name="sb_attention",
    )(qk, qk, vt, lmat)


def _moba_attn_kernel(q_ref, k_ref, vt_ref, avg_ref, o_ref, kmean_ref, *, blk, kb, ncb):
    i = pl.program_id(2)
    nb = vt_ref.shape[1]
    nbp = avg_ref.shape[0]
    cols = [slice(cb * LANES, (cb + 1) * LANES) for cb in range(ncb)]

    @pl.when(i == 0)
    def _():
        kmean_ref[...] = jnp.dot(avg_ref[...], k_ref[0], preferred_element_type=F32)

    q2 = [_stack_heads(q_ref[0, :, c], interleaved=True, scale=SOFTMAX_SCALE_LOG2) for c in cols]
    krow, qidx = _key_query_iota(blk)
    causal = krow <= qidx
    nidx = lax.broadcasted_iota(jnp.int32, (nbp, 2 * blk), 0)
    neg_inf = jnp.float32(-jnp.inf)
    neg_inf_b = jnp.asarray(-jnp.inf, BF16)

    picks = []
    for c, q in zip(cols, q2):
        km = jnp.concatenate(_split_bf16(kmean_ref[:, c]), axis=1)
        gate = lax.dot_general(km, jnp.concatenate([q, q], axis=1), NT_DIMS,
                               preferred_element_type=F32)
        gate = jnp.where(nidx < i, gate, neg_inf)
        chosen = []
        for _ in range(MOBA_TOPK):
            top = jnp.max(gate, axis=0, keepdims=True)
            idx = jnp.min(jnp.where(gate == top, nidx, nbp), axis=0, keepdims=True)
            chosen.append(jnp.where(top > neg_inf, idx, -1))
            gate = jnp.where(nidx == idx, neg_inf, gate)
        picks.append(chosen)

    def score_stage(ids, own_first):
        load_ids = [jnp.minimum(bid, nb - 1) for bid in ids]
        scores = []
        for cb, c in enumerate(cols):
            k_rows = jnp.concatenate(
                [k_ref[0, pl.ds(pl.multiple_of(bid * blk, blk), blk), c] for bid in load_ids], axis=0)
            s = lax.dot_general(k_rows, q2[cb], NT_DIMS, preferred_element_type=F32)
            parts = []
            for g, bid in enumerate(ids):
                if own_first and g == 0:
                    mask = causal
                else:
                    mask = functools.reduce(jnp.logical_or, [p == bid for p in picks[cb]])
                parts.append(jnp.where(mask, s[g * blk:(g + 1) * blk].astype(BF16), neg_inf_b))
            scores.append(jnp.concatenate(parts, axis=0))
        return load_ids, scores

    def softmax_stage(scores, st):
        soft = []
        for cb, s in enumerate(scores):
            m, l = st[4 * cb], st[4 * cb + 1]
            m_new = jnp.maximum(m, jnp.max(s, axis=0, keepdims=True).astype(F32))
            p = jnp.exp2(s - m_new.astype(BF16))
            alpha = jnp.exp2(m - m_new)
            soft.append((m_new, l, alpha, p))
        return soft

    def value_stage(load_ids, soft, st):
        out = []
        for cb, (c, (m_new, l_old, alpha, p)) in enumerate(zip(cols, soft)):
            v_cat = jnp.concatenate([vt_ref[0, bid, c, :] for bid in load_ids], axis=1)
            v_aug = jnp.concatenate([v_cat, jnp.ones((8, v_cat.shape[1]), BF16)], axis=0)
            o = jnp.dot(v_aug, p, preferred_element_type=F32)
            o_a, o_b = _own_heads(o[:2 * HEAD_DIM], blk)
            l_new = alpha * l_old + o[2 * HEAD_DIM:2 * HEAD_DIM + 1]
            out += [m_new, l_new,
                    alpha[:, :blk] * st[4 * cb + 2] + o_a, alpha[:, blk:] * st[4 * cb + 3] + o_b]
        return tuple(out)

    def step(ids, own_first, st):
        load_ids, scores = score_stage(ids, own_first)
        return value_stage(load_ids, softmax_stage(scores, st), st)

    def past_ids(t):
        return [(kb - 1) + t * kb + g for g in range(kb)]

    st = (jnp.full((1, 2 * blk), neg_inf, F32), jnp.zeros((1, 2 * blk), F32),
          jnp.zeros((HEAD_DIM, blk), F32), jnp.zeros((HEAD_DIM, blk), F32)) * ncb
    st = step([i] + list(range(kb - 1)), True, st)
    n_steps = (jnp.maximum(i - (kb - 1), 0) + kb - 1) // kb

    def two_steps(t2, st):
        ids_a, scores_a = score_stage(past_ids(2 * t2), False)
        soft_a = softmax_stage(scores_a, st)
        ids_b, scores_b = score_stage(past_ids(2 * t2 + 1), False)
        st = value_stage(ids_a, soft_a, st)
        return value_stage(ids_b, softmax_stage(scores_b, st), st)

    st = lax.fori_loop(0, n_steps // 2, two_steps, st)
    st = lax.fori_loop(0, n_steps % 2, lambda _, st: step(past_ids(n_steps - 1), False, st), st)
    for cb, c in enumerate(cols):
        l, acc_a, acc_b = st[4 * cb + 1:4 * cb + 4]
        o_ref[0, :, c] = jnp.concatenate([acc_a / l[:, :blk], acc_b / l[:, blk:]],
                                         axis=0).T.astype(o_ref.dtype)


def _moba_attention(qk, vt):
    b, s, _ = qk.shape
    blk = MOBA_BLOCK
    nb = s // blk
    nbp = -(-nb // 8) * 8
    ncb = MOBA_COL_BLOCKS
    blk_of_key = np.arange(s) // blk
    avg = jnp.asarray((np.arange(nbp)[:, None] == blk_of_key[None, :]) * (1.0 / blk), BF16)
    return pl.pallas_call(
        functools.partial(_moba_attn_kernel, blk=blk, kb=min(MOBA_TILE_BLOCKS, nb), ncb=ncb),
        out_shape=jax.ShapeDtypeStruct((b, s, D_MODEL), BF16),
        grid=(b, N_QBLK // ncb, nb),
        in_specs=[
            pl.BlockSpec((1, blk, ncb * LANES), lambda bi, c, i: (bi, i, c)),
            pl.BlockSpec((1, s, ncb * LANES), lambda bi, c, i: (bi, 0, N_QBLK // ncb + c)),
            pl.BlockSpec((1, nb, ncb * LANES, blk), lambda bi, c, i: (bi, 0, c, 0)),
            _resident((nbp, s)),
        ],
        out_specs=pl.BlockSpec((1, blk, ncb * LANES), lambda bi, c, i: (bi, i, c)),
        scratch_shapes=[pltpu.VMEM((nbp, ncb * LANES), F32)],
        compiler_params=_compiler_params(3),
        name="moba_attention",
    )(qk, qk, vt, avg)


def _swa_attn_kernel(sink_ref, q_ref, k_ref, vt_ref, o_ref, *, blk, nsub):
    c = pl.program_id(1)
    i = pl.program_id(2)
    krow, qidx = _key_query_iota(blk)
    in_prev = krow > qidx
    in_own = krow <= qidx
    head_b = lax.broadcasted_iota(jnp.int32, (1, 2 * blk), 1) >= blk
    sink = jnp.where(head_b, sink_ref[c + N_QBLK], sink_ref[c]) * LOG2E
    neg_inf = jnp.float32(-jnp.inf)

    blocks = [(i * nsub + u, jnp.maximum(i * nsub + u - 1, 0)) for u in range(nsub)]
    scores = []
    for u, (g, gp) in enumerate(blocks):
        q2 = _stack_heads(q_ref[0, u * blk:(u + 1) * blk, :], interleaved=True, scale=SOFTMAX_SCALE_LOG2)
        k_rows = jnp.concatenate(
            [k_ref[0, pl.ds(pl.multiple_of(gp * blk, blk), blk), :],
             k_ref[0, pl.ds(pl.multiple_of(g * blk, blk), blk), :]], axis=0)
        s = lax.dot_general(k_rows, q2, NT_DIMS, preferred_element_type=F32)
        scores.append(jnp.concatenate(
            [jnp.where(jnp.logical_and(in_prev, g > 0), s[:blk], neg_inf),
             jnp.where(in_own, s[blk:], neg_inf)], axis=0))
    probs = []
    for s in scores:
        m = jnp.maximum(jnp.max(s, axis=0, keepdims=True), sink)
        p = jnp.exp2(s - m)
        probs.append((p.astype(BF16), _colsum(p) + jnp.exp2(sink - m)))
    for u, ((g, gp), (p, denom)) in enumerate(zip(blocks, probs)):
        v_cat = jnp.concatenate([vt_ref[0, gp], vt_ref[0, g]], axis=1)
        o_a, o_b = _own_heads(jnp.dot(v_cat, p, preferred_element_type=F32), blk)
        _store_heads(o_ref, slice(u * blk, (u + 1) * blk), o_a / denom[:, :blk], o_b / denom[:, blk:])


def _swa_attention(qk, vt, sinks):
    b, s, _ = qk.shape
    blk = SWA_WINDOW
    nb = s // blk
    tq = min(SWA_Q_TILE, s)
    return pl.pallas_call(
        functools.partial(_swa_attn_kernel, blk=blk, nsub=tq // blk),
        out_shape=jax.ShapeDtypeStruct((b, s, D_MODEL), BF16),
        grid=(b, N_QBLK, s // tq),
        in_specs=[
            pl.BlockSpec(memory_space=pltpu.SMEM),
            pl.BlockSpec((1, tq, LANES), lambda bi, c, i: (bi, i, c)),
            pl.BlockSpec((1, s, LANES), lambda bi, c, i: (bi, 0, N_QBLK)),
            pl.BlockSpec((1, nb, LANES, blk), lambda bi, c, i: (bi, 0, 0, 0)),
        ],
        out_specs=pl.BlockSpec((1, tq, LANES), lambda bi, c, i: (bi, i, c)),
        compiler_params=_compiler_params(3),
        name="swa_attention",
    )(sinks, qk, qk, vt)


def _out_ffn_kernel(attn_ref, x_ref, mod_ref, gain_ref, wo_ref, wg_ref, wu_ref, wd_ref, o_ref):
    y = jnp.dot(attn_ref[0], wo_ref[...], preferred_element_type=F32)
    x1 = x_ref[0] + mod_ref[0, 0, 2:3, :] * y
    h = _norm_modulate(x1, gain_ref[...], mod_ref[0, 0, 3:4, :], mod_ref[0, 0, 4:5, :])
    hb = h.astype(BF16)
    dff = wg_ref.shape[1]
    n_tiles = dff // MXU_WIDTH
    edges = [MXU_WIDTH * ((n_tiles * ck + FFN_CHUNKS - 1) // FFN_CHUNKS) for ck in range(FFN_CHUNKS)] + [dff]
    ff = None
    for ck in range(FFN_CHUNKS):
        cs = slice(edges[ck], edges[ck + 1])
        gate = jnp.dot(hb, wg_ref[:, cs], preferred_element_type=F32)
        up = jnp.dot(hb, wu_ref[:, cs], preferred_element_type=F32)
        act = (gate * (1.0 / (1.0 + jnp.exp(-gate)))) * up
        part = jnp.dot(act.astype(BF16), wd_ref[cs, :], preferred_element_type=F32)
        ff = part if ff is None else ff + part
    o_ref[0] = x1 + mod_ref[0, 0, 5:6, :] * ff


def _out_ffn(attn, x, mod, layer, gain, wo, wg, wu, wd):
    b, s, d = x.shape
    dff = wg.shape[1]
    tm = min(FFN_ROW_TILE, s)
    return pl.pallas_call(
        _out_ffn_kernel,
        out_shape=jax.ShapeDtypeStruct((b, s, d), F32),
        grid=(b, s // tm),
        in_specs=[
            pl.BlockSpec((1, tm, d), lambda bi, i: (bi, i, 0)),
            pl.BlockSpec((1, tm, d), lambda bi, i: (bi, i, 0)),
            pl.BlockSpec((1, 1, 6, d), lambda bi, i: (layer, bi, 0, 0)),
            _resident((1, d)),
            _resident((d, d)),
            _resident((d, dff)),
            _resident((d, dff)),
            _resident((dff, d)),
        ],
        out_specs=pl.BlockSpec((1, tm, d), lambda bi, i: (bi, i, 0)),
        compiler_params=_compiler_params(2),
        name=f"out_ffn_l{layer}",
    )(attn, x, mod, gain.reshape(1, d), wo, wg, wu, wd)


def _interleave_perm(head_a, head_b):
    half = HEAD_DIM // 2
    r = np.arange(half)
    return np.concatenate([head_a * HEAD_DIM + r, head_b * HEAD_DIM + r,
                           head_a * HEAD_DIM + half + r, head_b * HEAD_DIM + half + r])


def _rope_tables(seq_len):
    inv_freq = 1.0 / (ROPE_THETA ** (jnp.arange(0, HEAD_DIM, 2, dtype=F32) / HEAD_DIM))
    ang = jnp.arange(seq_len, dtype=F32)[:, None] * inv_freq[None, :]
    cos, sin = jnp.cos(ang), jnp.sin(ang)
    return (jnp.concatenate([cos] * 4, axis=1),
            jnp.concatenate([-sin, -sin, sin, sin], axis=1))


def _head_sum_matrix():
    half = HEAD_DIM // 2
    head = (np.arange(LANES) // half) % 2
    g = (head[:, None] == head[None, :]).astype(np.float32)
    return jnp.asarray(np.concatenate([g, g], axis=0), BF16)


def _block_gain(qk_gain):
    half = HEAD_DIM // 2
    lane = np.arange(LANES)
    dim = (lane % half) + half * (lane // (2 * half))
    return qk_gain[:, dim]


def kernel(x, c, ada_w, ada_b, norm_gain, ffn_w_gate, ffn_w_up, ffn_w_down, sb_w_in, sb_w_out, moba_w_in, moba_qk_gain, moba_w_out, swa_w_in, swa_qk_gain, swa_sinks, swa_w_out):
    b, s, d = x.shape
    depth = ada_w.shape[0]
    hd_all = N_HEADS * HEAD_DIM
    mod = _ada_modulation(c, ada_w, ada_b).reshape(depth, b, 6, d)
    cos_t, sin_t = _rope_tables(s)
    gsum = _head_sum_matrix()

    moba_cols = np.concatenate(
        [_interleave_perm(2 * cb, 2 * cb + 1) for cb in range(N_QBLK)]
        + [hd_all + _interleave_perm(2 * cb, 2 * cb + 1) for cb in range(N_QBLK)])
    swa_cols = np.concatenate(
        [_interleave_perm(cb, N_QBLK + cb) for cb in range(N_QBLK)]
        + [hd_all + _interleave_perm(0, 1)])
    swa_out_rows = np.concatenate(
        [np.concatenate([np.arange(HEAD_DIM) + cb * HEAD_DIM,
                         np.arange(HEAD_DIM) + (N_QBLK + cb) * HEAD_DIM]) for cb in range(N_QBLK)])

    for i in range(depth):
        kind, j = i % N_MIXERS, i // N_MIXERS
        if kind == 0:
            w_in = sb_w_in[j]
            wqk = w_in[:, :2 * hd_all].astype(BF16)
            wvt = w_in[:, 2 * hd_all:].T.astype(BF16)
            qk, vt = _norm_proj(x, mod, i, norm_gain[i, 0], wqk, wvt, SB_BLOCK)
            attn = _sb_attention(qk, vt, SB_BLOCK)
            wo = sb_w_out[j].astype(BF16)
        elif kind == 1:
            w_in = moba_w_in[j]
            wqk = w_in[:, moba_cols].astype(BF16)
            wvt = w_in[:, 2 * hd_all:].T.astype(BF16)
            rope = (cos_t, sin_t, gsum, _block_gain(moba_qk_gain[j]))
            qk, vt = _norm_proj(x, mod, i, norm_gain[i, 0], wqk, wvt, MOBA_BLOCK, rope)
            attn = _moba_attention(qk, vt)
            wo = moba_w_out[j].astype(BF16)
        else:
            w_in = swa_w_in[j]
            wqk = w_in[:, swa_cols].astype(BF16)
            wvt = w_in[:, hd_all + SWA_KV_HEADS * HEAD_DIM:].T.astype(BF16)
            rope = (cos_t, sin_t, gsum, _block_gain(swa_qk_gain[j]))
            qk, vt = _norm_proj(x, mod, i, norm_gain[i, 0], wqk, wvt, SWA_WINDOW, rope)
            attn = _swa_attention(qk, vt, swa_sinks[j])
            wo = swa_w_out[j][swa_out_rows, :].astype(BF16)
        x = _out_ffn(attn, x, mod, i, norm_gain[i, 1], wo,
                     ffn_w_gate[i].astype(BF16), ffn_w_up[i].astype(BF16), ffn_w_down[i].astype(BF16))
    return x
```
